```python
import math, functools
import jax, jax.numpy as jnp
from jax import lax
import numpy as np

D_MODEL = 2048
BATCH = 8
SEQ = 2048
DEPTH = 1
DEC_BATCH = 128
DEC_SEQ = 8
PAST_LEN = 16384
PAGE_SIZE = 128

HG_HEADS = 8
HG_DK = 128
HG_DV = 128
HG_WIDTH = HG_HEADS * HG_DV
HG_CHUNK = 64
MLA_HEADS = 8
MLA_Q_LORA = 512
MLA_KV_LORA = 256
MLA_NOPE = 128
MLA_ROPE = 64
MLA_V = 128
MLA_WIDTH = MLA_HEADS * MLA_V
MLA_SCALE = (MLA_NOPE + MLA_ROPE) ** -0.5
ROPE_THETA = 10000.0
MIX_WIDTH = HG_WIDTH + MLA_WIDTH
IN_WIDTH = 2 * HG_HEADS * HG_DK + 2 * HG_WIDTH + MLA_Q_LORA + MLA_KV_LORA + MLA_ROPE
D_FF = 5632
CONV_W = 3
PLE_DIM = 256
ATTN_BLOCK = 128
EPS = 1e-6
NEG_INF = -1e30

kernel_name = 'hybrid_hgrn2_mla_convffn_step'


def rmsnorm(x, g):
    xf = x.astype(jnp.float32)
    xf = xf * lax.rsqrt(jnp.mean(xf * xf, axis=-1, keepdims=True) + EPS)
    return xf.astype(x.dtype) * g.astype(x.dtype)


def rope_tables(pos):
    half = MLA_ROPE // 2
    inv = 1.0 / (ROPE_THETA ** (jnp.arange(half, dtype=jnp.float32) / half))
    ang = pos.astype(jnp.float32)[:, None] * inv[None, :]
    return jnp.cos(ang), jnp.sin(ang)


def apply_rope(x, cos, sin):
    x1, x2 = jnp.split(x.astype(jnp.float32), 2, axis=-1)
    return jnp.concatenate([x1 * cos - x2 * sin, x2 * cos + x1 * sin], axis=-1).astype(x.dtype)


def hgrn2_recurrence(q, k, v, logf, s0, chunk):
    b, t, h, dk = q.shape
    dv = v.shape[-1]
    n = t // chunk

    def to_chunks(a):
        return jnp.moveaxis(a.astype(jnp.float32).reshape(b, n, chunk, h, a.shape[-1]), 1, 0)

    causal = jnp.tril(jnp.ones((chunk, chunk), dtype=bool))

    def step(s, blk):
        qc, kc, vc, lf = blk
        cum = jnp.cumsum(lf, axis=1)
        q_dec = qc * jnp.exp(cum)
        k_inv = kc * jnp.exp(-cum)
        a = jnp.einsum('bthk,bshk->bhts', q_dec, k_inv)
        a = jnp.where(causal, a, 0.0)
        o = jnp.einsum('bhts,bshv->bthv', a, vc) + jnp.einsum('bthk,bhkv->bthv', q_dec, s)
        total = cum[:, -1]
        k_end = kc * jnp.exp(total[:, None] - cum)
        s_new = jnp.exp(total)[..., None] * s + jnp.einsum('bshk,bshv->bhkv', k_end, vc)
        return s_new, o

    s_final, o = lax.scan(step, s0.astype(jnp.float32),
                          (to_chunks(q), to_chunks(k), to_chunks(v), to_chunks(logf)))
    o = jnp.moveaxis(o, 0, 1).reshape(b, t, h, dv)
    return o, s_final


def hgrn2_mixer(q_raw, f_raw, i_raw, g_raw, lb, onorm, s0):
    b, t, _ = q_raw.shape
    q = jax.nn.silu(q_raw.astype(jnp.float32)).reshape(b, t, HG_HEADS, HG_DK)
    fx = f_raw.astype(jnp.float32)
    f = lb + (1.0 - lb) * jax.nn.sigmoid(fx)
    k = ((1.0 - lb) * jax.nn.sigmoid(-fx)).reshape(b, t, HG_HEADS, HG_DK)
    logf = jnp.log(f).reshape(b, t, HG_HEADS, HG_DK)
    v = i_raw.reshape(b, t, HG_HEADS, HG_DV)
    o, s = hgrn2_recurrence(q, k, v, logf, s0, math.gcd(t, HG_CHUNK))
    o = rmsnorm(o, onorm.reshape(HG_HEADS, HG_DV))
    o = o * jax.nn.sigmoid(g_raw.astype(jnp.float32)).reshape(b, t, HG_HEADS, HG_DV)
    return o.reshape(b, t, HG_WIDTH).astype(q_raw.dtype), s.astype(q_raw.dtype)


def mla_project(cq_raw, ckv_raw, kr_raw, q_norm, kv_norm, w_q_b, w_kv_b, pos):
    b, t, _ = cq_raw.shape
    cq = rmsnorm(cq_raw, q_norm)
    q = (cq @ w_q_b).reshape(b, t, MLA_HEADS, MLA_NOPE + MLA_ROPE)
    q_nope, q_rope = q[..., :MLA_NOPE], q[..., MLA_NOPE:]
    cos, sin = rope_tables(pos)
    q_rope = apply_rope(q_rope, cos[:, None, :], sin[:, None, :]) * MLA_SCALE
    k_rope = apply_rope(kr_raw, cos, sin)
    ckv = rmsnorm(ckv_raw, kv_norm)
    wkv = w_kv_b.reshape(MLA_KV_LORA, MLA_HEADS, MLA_NOPE + MLA_V)
    w_uk, w_uv = wkv[..., :MLA_NOPE], wkv[..., MLA_NOPE:]
    q_lat = jnp.einsum('bthn,chn->bthc', q_nope, w_uk) * MLA_SCALE
    return q_lat, q_rope, ckv, k_rope, w_uv


def mla_prompt_attention(q_lat, q_rope, ckv, k_rope):
    b, t, h, c = q_lat.shape
    nb = t // ATTN_BLOCK
    key_pos = jnp.arange(t)
    ckv32 = ckv.astype(jnp.float32)
    kr32 = k_rope.astype(jnp.float32)

    def block(args):
        ql, qr, i = args
        s = (jnp.einsum('bqhc,bsc->bhqs', ql.astype(jnp.float32), ckv32)
             + jnp.einsum('bqhr,bsr->bhqs', qr.astype(jnp.float32), kr32))
        qpos = i * ATTN_BLOCK + jnp.arange(ATTN_BLOCK)
        s = jnp.where(key_pos[None, :] <= qpos[:, None], s, NEG_INF)
        p = jax.nn.softmax(s, axis=-1)
        return jnp.einsum('bhqs,bsc->bqhc', p, ckv32)

    qb = jnp.moveaxis(q_lat.reshape(b, nb, ATTN_BLOCK, h, c), 1, 0)
    rb = jnp.moveaxis(q_rope.reshape(b, nb, ATTN_BLOCK, h, MLA_ROPE), 1, 0)
    out = lax.map(block, (qb, rb, jnp.arange(nb)))
    return jnp.moveaxis(out, 0, 1).reshape(b, t, h, c).astype(q_lat.dtype)


def mla_sample_attention(q_lat, q_rope, ckv, k_rope, cache_ckv, cache_krope, page_table, layer):
    b, t, h, c = q_lat.shape
    ql = q_lat.astype(jnp.float32)
    qr = q_rope.astype(jnp.float32)

    def page_step(carry, pids):
        m, l, acc = carry
        kc = cache_ckv[layer, pids].astype(jnp.float32)
        kr = cache_krope[layer, pids].astype(jnp.float32)
        s = jnp.einsum('bthc,bpc->bhtp', ql, kc) + jnp.einsum('bthr,bpr->bhtp', qr, kr)
        m_new = jnp.maximum(m, s.max(axis=-1))
        corr = jnp.exp(m - m_new)
        p = jnp.exp(s - m_new[..., None])
        l = l * corr + p.sum(axis=-1)
        acc = acc * corr[..., None] + jnp.einsum('bhtp,bpc->bhtc', p, kc)
        return (m_new, l, acc), None

    init = (jnp.full((b, h, t), NEG_INF, jnp.float32), jnp.zeros((b, h, t), jnp.float32),
            jnp.zeros((b, h, t, c), jnp.float32))
    (m, l, acc), _ = lax.scan(page_step, init, page_table.T)
    kc = ckv.astype(jnp.float32)
    kr = k_rope.astype(jnp.float32)
    s = jnp.einsum('bthc,bsc->bhts', ql, kc) + jnp.einsum('bthr,bsr->bhts', qr, kr)
    s = jnp.where(jnp.tril(jnp.ones((t, t), dtype=bool)), s, NEG_INF)
    m_new = jnp.maximum(m, s.max(axis=-1))
    corr = jnp.exp(m - m_new)
    p = jnp.exp(s - m_new[..., None])
    l = l * corr + p.sum(axis=-1)
    acc = acc * corr[..., None] + jnp.einsum('bhts,bsc->bhtc', p, kc)
    out = acc / l[..., None]
    return jnp.transpose(out, (0, 2, 1, 3)).astype(q_lat.dtype)


def conv_ffn(h, conv_s0, w_up, conv_w, conv_b, w_down):
    t = h.shape[1]
    u = h @ w_up
    a, gate_b = u[..., :D_FF], u[..., D_FF:]
    a_ext = jnp.concatenate([conv_s0.astype(a.dtype), a], axis=1)
    conv = conv_b
    for j in range(CONV_W):
        conv = conv + conv_w[j] * a_ext[:, j:j + t]
    y = (jax.nn.silu(conv) * gate_b) @ w_down
    return y, a_ext[:, -(CONV_W - 1):]


def trunk_layer(x, p_l, pos, hg_s0, conv_s0, attend, norm_mix, w_in, lb, hg_onorm,
                mla_q_norm, w_q_b, mla_kv_norm, w_kv_b, w_out, norm_ffn, w_up, conv_w,
                conv_b, w_down, norm_ple, w_ple_gate, w_ple_proj):
    b, t, _ = x.shape
    proj = rmsnorm(x, norm_mix) @ w_in
    o1 = HG_HEADS * HG_DK
    o2 = 2 * o1
    o3 = o2 + HG_WIDTH
    o4 = o3 + HG_WIDTH
    o5 = o4 + MLA_Q_LORA
    o6 = o5 + MLA_KV_LORA
    o_hg, hg_s = hgrn2_mixer(proj[..., :o1], proj[..., o1:o2], proj[..., o2:o3],
                             proj[..., o3:o4], lb, hg_onorm, hg_s0)
    q_lat, q_rope, ckv, k_rope, w_uv = mla_project(proj[..., o4:o5], proj[..., o5:o6],
                                                   proj[..., o6:], mla_q_norm, mla_kv_norm,
                                                   w_q_b, w_kv_b, pos)
    lat = attend(q_lat, q_rope, ckv, k_rope)
    o_mla = jnp.einsum('bthc,chv->bthv', lat, w_uv).reshape(b, t, MLA_WIDTH)
    x = x + jnp.concatenate([o_hg, o_mla.astype(x.dtype)], axis=-1) @ w_out
    y, conv_s = conv_ffn(rmsnorm(x, norm_ffn), conv_s0, w_up, conv_w, conv_b, w_down)
    x = x + y
    x = x + jax.nn.sigmoid(rmsnorm(x, norm_ple) @ w_ple_gate) * (p_l @ w_ple_proj)
    return x, ckv, k_rope, hg_s, conv_s


def setup_inputs(seed: int = 0) -> dict:
    key = jax.random.key(seed)
    ks = jax.random.split(key, 28)
    f32 = jnp.float32
    n_pages = PAST_LEN // PAGE_SIZE
    n_used = DEC_BATCH * n_pages
    n_pool = n_used + n_used // 4

    def nrm(k, shape, scale):
        return scale * jax.random.normal(k, shape, f32)

    def gain(k, shape):
        return 1.0 + 0.05 * jax.random.normal(k, shape, f32)

    page_table = jax.random.permutation(ks[0], n_pool)[:n_used].reshape(DEC_BATCH, n_pages).astype(jnp.int32)
    return {
        'x_prompt': nrm(ks[1], (BATCH, SEQ, D_MODEL), 1.0),
        'x_sample': nrm(ks[2], (DEC_BATCH, DEC_SEQ, D_MODEL), 1.0),
        'cache_ckv': nrm(ks[3], (DEPTH, n_pool, PAGE_SIZE, MLA_KV_LORA), 1.0),
        'cache_krope': nrm(ks[4], (DEPTH, n_pool, PAGE_SIZE, MLA_ROPE), 1.0),
        'state_hgrn': nrm(ks[5], (DEPTH, DEC_BATCH, HG_HEADS, HG_DK, HG_DV), 0.5),
        'state_conv': nrm(ks[6], (DEPTH, DEC_BATCH, CONV_W - 1, D_FF), 1.0),
        'page_table': page_table,
        'p_prompt': nrm(ks[7], (DEPTH, BATCH, SEQ, PLE_DIM), 1.0),
        'p_sample': nrm(ks[8], (DEPTH, DEC_BATCH, DEC_SEQ, PLE_DIM), 1.0),
        'norm_mix': gain(ks[9], (DEPTH, D_MODEL)),
        'w_in': nrm(ks[10], (DEPTH, D_MODEL, IN_WIDTH), D_MODEL ** -0.5),
        'hg_lower': nrm(ks[11], (DEPTH + 1, HG_HEADS * HG_DK), 0.5),
        'hg_onorm': gain(ks[12], (DEPTH, HG_WIDTH)),
        'mla_q_norm': gain(ks[13], (DEPTH, MLA_Q_LORA)),
        'w_q_b': nrm(ks[14], (DEPTH, MLA_Q_LORA, MLA_HEADS * (MLA_NOPE + MLA_ROPE)), MLA_Q_LORA ** -0.5),
        'mla_kv_norm': gain(ks[15], (DEPTH, MLA_KV_LORA)),
        'w_kv_b': nrm(ks[16], (DEPTH, MLA_KV_LORA, MLA_HEADS * (MLA_NOPE + MLA_V)), MLA_KV_LORA ** -0.5),
        'w_out': nrm(ks[17], (DEPTH, MIX_WIDTH, D_MODEL), MIX_WIDTH ** -0.5),
        'norm_ffn': gain(ks[18], (DEPTH, D_MODEL)),
        'w_up': nrm(ks[19], (DEPTH, D_MODEL, 2 * D_FF), D_MODEL ** -0.5),
        'conv_w': nrm(ks[20], (DEPTH, CONV_W, D_FF), CONV_W ** -0.5),
        'conv_b': nrm(ks[21], (DEPTH, D_FF), 0.02),
        'w_down': nrm(ks[22], (DEPTH, D_FF, D_MODEL), D_FF ** -0.5),
        'norm_ple': gain(ks[23], (DEPTH, D_MODEL)),
        'w_ple_gate': nrm(ks[24], (DEPTH, D_MODEL, D_MODEL), D_MODEL ** -0.5),
        'w_ple_proj': nrm(ks[25], (DEPTH, PLE_DIM, D_MODEL), PLE_DIM ** -0.5),
        'norm_final': gain(ks[26], (D_MODEL,)),
    }


def reference(x_prompt, x_sample, cache_ckv, cache_krope, state_hgrn, state_conv, page_table,
              p_prompt, p_sample, norm_mix, w_in, hg_lower, hg_onorm, mla_q_norm, w_q_b,
              mla_kv_norm, w_kv_b, w_out, norm_ffn, w_up, conv_w, conv_b, w_down, norm_ple,
              w_ple_gate, w_ple_proj, norm_final):
    lb_all = jnp.cumsum(jax.nn.softmax(hg_lower.astype(jnp.float32), axis=0), axis=0)
    bp, tp, _ = x_prompt.shape
    pos_prompt = jnp.arange(tp, dtype=jnp.int32)
    pos_sample = PAST_LEN + jnp.arange(x_sample.shape[1], dtype=jnp.int32)
    hg0_prompt = jnp.zeros((bp, HG_HEADS, HG_DK, HG_DV), jnp.float32)
    conv0_prompt = jnp.zeros((bp, CONV_W - 1, D_FF), x_prompt.dtype)
    xp, xs = x_prompt, x_sample
    new_p, new_s = [], []
    for l in range(DEPTH):
        w = (norm_mix[l], w_in[l], lb_all[l], hg_onorm[l], mla_q_norm[l], w_q_b[l],
             mla_kv_norm[l], w_kv_b[l], w_out[l], norm_ffn[l], w_up[l], conv_w[l],
             conv_b[l], w_down[l], norm_ple[l], w_ple_gate[l], w_ple_proj[l])
        xp, ckv_p, kr_p, hg_p, cv_p = trunk_layer(xp, p_prompt[l], pos_prompt, hg0_prompt,
                                                  conv0_prompt, mla_prompt_attention, *w)
        attend_s = functools.partial(mla_sample_attention, cache_ckv=cache_ckv,
                                     cache_krope=cache_krope, page_table=page_table, layer=l)
        xs, ckv_s, kr_s, hg_s, cv_s = trunk_layer(xs, p_sample[l], pos_sample, state_hgrn[l],
                                                  state_conv[l], attend_s, *w)
        new_p.append((ckv_p, kr_p, hg_p, cv_p))
        new_s.append((ckv_s, kr_s, hg_s, cv_s))
    y_prompt = rmsnorm(xp, norm_final)
    y_sample = rmsnorm(xs, norm_final)
    ckv_prompt = jnp.stack([e[0] for e in new_p])
    krope_prompt = jnp.stack([e[1] for e in new_p])
    hgrn_prompt = jnp.stack([e[2] for e in new_p])
    conv_prompt = jnp.stack([e[3] for e in new_p])
    ckv_sample = jnp.stack([e[0] for e in new_s])
    krope_sample = jnp.stack([e[1] for e in new_s])
    hgrn_sample = jnp.stack([e[2] for e in new_s])
    conv_sample = jnp.stack([e[3] for e in new_s])
    return (y_prompt, y_sample, ckv_prompt, krope_prompt, hgrn_prompt, conv_prompt,
            ckv_sample, krope_sample, hgrn_sample, conv_sample)
```

```python
import functools

import jax
import jax.numpy as jnp
from jax import lax
from jax.experimental import pallas as pl
from jax.experimental.pallas import tpu as pltpu

F32 = jnp.float32
BF16 = jnp.bfloat16

HG_HEADS = 8
HG_DK = 128
HG_DV = 128
HG_CHUNK = 64
MLA_HEADS = 8
MLA_Q_LORA = 512
MLA_KV_LORA = 256
MLA_NOPE = 128
MLA_ROPE = 64
MLA_V = 128
MLA_SCALE = (MLA_NOPE + MLA_ROPE) ** -0.5
ROPE_THETA = 10000.0
PAST_LEN = 16384
CONV_W = 3
EPS = 1e-6
NEG_INF = -1e30

HG_COLS = 4 * HG_HEADS * HG_DK
MLA_COLS = 1024
Q_HEAD_COLS = 256
QK_PAD = 384
LANES = 128
SUBLANES = 8

VMEM_LIMIT = 48 * 1024 * 1024

_NT = (((1,), (1,)), ((), ()))
_TN = (((0,), (0,)), ((), ()))


def _cparams(*sem):
    return pltpu.CompilerParams(dimension_semantics=sem, vmem_limit_bytes=VMEM_LIMIT)


def _tile(n, pref):
    t = min(n, pref)
    assert n % t == 0, (n, pref)
    return t


def _rms_rows(x, g):
    ms = jnp.mean(x * x, axis=-1, keepdims=True)
    return (x * lax.rsqrt(ms + EPS)) * g


def _dot(a, b):
    return jnp.dot(a, b, preferred_element_type=F32)


def _dot_nt(a, b):
    return lax.dot_general(a, b, _NT, preferred_element_type=F32)


def _in_proj_kernel(x_ref, g_ref, w_ref, o_ref, h_ref):
    @pl.when(pl.program_id(1) == 0)
    def _():
        h_ref[...] = _rms_rows(x_ref[...], g_ref[...]).astype(BF16)

    o_ref[...] = _dot(h_ref[...], w_ref[...])


def _in_proj(x, g, w):
    n, d = x.shape
    cols = w.shape[1]
    tm = _tile(n, 512)
    tn = 1024
    return pl.pallas_call(
        _in_proj_kernel,
        grid=(n // tm, cols // tn),
        in_specs=[
            pl.BlockSpec((tm, d), lambda i, j: (i, 0)),
            pl.BlockSpec((1, d), lambda i, j: (0, 0)),
            pl.BlockSpec((d, tn), lambda i, j: (0, j)),
        ],
        out_specs=pl.BlockSpec((tm, tn), lambda i, j: (i, j)),
        out_shape=jax.ShapeDtypeStruct((n, cols), F32),
        scratch_shapes=[pltpu.VMEM((tm, d), BF16)],
        compiler_params=_cparams("arbitrary", "arbitrary"),
    )(x, g, w)


def _mla_proj_kernel(p_ref, cs_ref, qn_ref, kvn_ref, wq_ref, wuk_ref,
                     ckv_ref, kr_ref, kcat_ref, q_ref):
    p = p_ref[...]
    cs = cs_ref[...]
    tm = p.shape[0]
    lane = lax.broadcasted_iota(jnp.int32, (tm, LANES), 1)
    cq = _rms_rows(p[:, :MLA_Q_LORA], qn_ref[...]).astype(BF16)
    ckv = _rms_rows(p[:, MLA_Q_LORA:MLA_Q_LORA + MLA_KV_LORA], kvn_ref[...])
    ckv_ref[...] = ckv
    kk = p[:, 768:896] * cs
    kro = kk + pltpu.roll(kk, MLA_ROPE, 1)
    kr_ref[...] = kro[:, :MLA_ROPE]
    kcat_ref[:, :MLA_KV_LORA] = ckv.astype(kcat_ref.dtype)
    kcat_ref[:, MLA_KV_LORA:] = jnp.where(lane < MLA_ROPE, kro, 0.0).astype(kcat_ref.dtype)
    q = _dot(cq, wq_ref[...])
    for h in range(MLA_HEADS):
        qh = q[:, h * Q_HEAD_COLS:(h + 1) * Q_HEAD_COLS]
        qlat = _dot(qh[:, :MLA_NOPE].astype(BF16), wuk_ref[h]) * MLA_SCALE
        rr = qh[:, MLA_NOPE:] * cs
        rope = (rr + pltpu.roll(rr, MLA_ROPE, 1)) * MLA_SCALE
        q_ref[:, h * QK_PAD:h * QK_PAD + MLA_KV_LORA] = qlat.astype(q_ref.dtype)
        q_ref[:, h * QK_PAD + MLA_KV_LORA:(h + 1) * QK_PAD] = (
            jnp.where(lane < MLA_ROPE, rope, 0.0).astype(q_ref.dtype))


def _mla_proj(proj, cs, qn, kvn, wq, wuk, q_dtype):
    n = proj.shape[0]
    tm = _tile(n, 256)
    cs_blocks = cs.shape[0] // tm
    mla_blk = HG_COLS // MLA_COLS
    return pl.pallas_call(
        _mla_proj_kernel,
        grid=(n // tm,),
        in_specs=[
            pl.BlockSpec((tm, MLA_COLS), lambda i: (i, mla_blk)),
            pl.BlockSpec((tm, LANES), lambda i: (i % cs_blocks, 0)),
            pl.BlockSpec((1, MLA_Q_LORA), lambda i: (0, 0)),
            pl.BlockSpec((1, MLA_KV_LORA), lambda i: (0, 0)),
            pl.BlockSpec(wq.shape, lambda i: (0, 0)),
            pl.BlockSpec(wuk.shape, lambda i: (0, 0, 0)),
        ],
        out_specs=[
            pl.BlockSpec((tm, MLA_KV_LORA), lambda i: (i, 0)),
            pl.BlockSpec((tm, MLA_ROPE), lambda i: (i, 0)),
            pl.BlockSpec((tm, QK_PAD), lambda i: (i, 0)),
            pl.BlockSpec((tm, MLA_HEADS * QK_PAD), lambda i: (i, 0)),
        ],
        out_shape=[
            jax.ShapeDtypeStruct((n, MLA_KV_LORA), F32),
            jax.ShapeDtypeStruct((n, MLA_ROPE), F32),
            jax.ShapeDtypeStruct((n, QK_PAD), BF16),
            jax.ShapeDtypeStruct((n, MLA_HEADS * QK_PAD), q_dtype),
        ],
        compiler_params=_cparams("arbitrary"),
    )(proj, cs, qn, kvn, wq, wuk)


def _hgrn_kernel(*refs, layer, nseq, nchunk, chunk, has_s0):
    if has_s0:
        hl_ref, q_ref, f_ref, i_ref, g_ref, on_ref, s0_ref, o_ref, sout_ref = refs
    else:
        hl_ref, q_ref, f_ref, i_ref, g_ref, on_ref, o_ref, sout_ref = refs
        s0_ref = None
    hl = hl_ref[...]
    e = jnp.exp(hl - jnp.max(hl, axis=0, keepdims=True))
    lb = jnp.sum(e[:layer + 1], axis=0, keepdims=True) / jnp.sum(e, axis=0, keepdims=True)
    onorm = on_ref[...]
    rix = lax.broadcasted_iota(jnp.int32, (chunk, chunk), 0)
    cix = lax.broadcasted_iota(jnp.int32, (chunk, chunk), 1)
    causal = rix >= cix
    tri = causal.astype(F32)

    def chunk_body(r0, st):
        qr = q_ref[pl.ds(r0, chunk), :]
        fx = f_ref[pl.ds(r0, chunk), :]
        v = i_ref[pl.ds(r0, chunk), :].astype(BF16)
        gr = g_ref[pl.ds(r0, chunk), :]
        f = lb + (1.0 - lb) * jax.nn.sigmoid(fx)
        k = (1.0 - lb) * jax.nn.sigmoid(-fx)
        cum = jnp.dot(tri, jnp.log(f), precision=lax.Precision.HIGHEST,
                      preferred_element_type=F32)
        qd = ((qr * jax.nn.sigmoid(qr)) * jnp.exp(cum)).astype(BF16)
        ki = (k * jnp.exp(-cum)).astype(BF16)
        a = jnp.where(causal, _dot_nt(qd, ki), 0.0)
        o = _dot(a.astype(BF16), v) + _dot_nt(qd, st.astype(BF16))
        total = cum[chunk - 1:chunk, :]
        ke = (k * jnp.exp(total - cum)).astype(BF16)
        st = st * jnp.exp(total) + lax.dot_general(v, ke, _TN, preferred_element_type=F32)
        out = _rms_rows(o, onorm) * jax.nn.sigmoid(gr)
        o_ref[pl.ds(r0, chunk), :] = out.astype(o_ref.dtype)
        return st

    def seq_body(sq, carry):
        if has_s0:
            st0 = s0_ref[sq, 0].T
        else:
            st0 = jnp.zeros((HG_DV, HG_DK), F32)
        base = sq * (nchunk * chunk)
        st = lax.fori_loop(
            0, nchunk,
            lambda c, st: chunk_body(pl.multiple_of(base + c * chunk, chunk), st), st0)
        sout_ref[sq, 0] = st.T
        return carry

    lax.fori_loop(0, nseq, seq_body, 0)


def _hgrn(proj, hg_lower, onorm, s0, *, layer, nb, t, seq_per_step, out_dtype):
    n = proj.shape[0]
    assert n == nb * t
    chunk = min(t, HG_CHUNK)
    assert t % chunk == 0 and nb % seq_per_step == 0
    rows = seq_per_step * t
    nl = hg_lower.shape[0]

    def col(seg):
        return pl.BlockSpec((rows, HG_DK), lambda i, h: (i, seg * HG_HEADS + h))

    in_specs = [
        pl.BlockSpec((nl, HG_DK), lambda i, h: (0, h)),
        col(0), col(1), col(2), col(3),
        pl.BlockSpec((1, HG_DV), lambda i, h: (0, h)),
    ]
    args = [hg_lower, proj, proj, proj, proj, onorm]
    state_spec = pl.BlockSpec((seq_per_step, 1, HG_DK, HG_DV), lambda i, h: (i, h, 0, 0))
    if s0 is not None:
        in_specs.append(state_spec)
        args.append(s0)
    kern = functools.partial(_hgrn_kernel, layer=layer, nseq=seq_per_step, nchunk=t // chunk,
                             chunk=chunk, has_s0=s0 is not None)
    return pl.pallas_call(
        kern,
        grid=(nb // seq_per_step, HG_HEADS),
        in_specs=in_specs,
        out_specs=[
            pl.BlockSpec((rows, HG_DV), lambda i, h: (i, h)),
            state_spec,
        ],
        out_shape=[
            jax.ShapeDtypeStruct((n, HG_HEADS * HG_DV), out_dtype),
            jax.ShapeDtypeStruct((nb, HG_HEADS, HG_DK, HG_DV), F32),
        ],
        compiler_params=_cparams("arbitrary", "arbitrary"),
    )(*args)


def _prompt_attn_kernel(q_ref, k_ref, wuv_ref, o_ref, qs_ref, m_ref, l_ref, acc_ref, *, tq):
    qi = pl.program_id(1)
    rows = MLA_HEADS * tq
    for h in range(MLA_HEADS):
        qs_ref[h * tq:(h + 1) * tq, :] = q_ref[:, h * QK_PAD:(h + 1) * QK_PAD]
    m_ref[...] = jnp.full((rows, 1), NEG_INF, F32)
    l_ref[...] = jnp.zeros((rows, 1), F32)
    acc_ref[...] = jnp.zeros((rows, MLA_KV_LORA), F32)
    qpos = qi * tq + lax.rem(lax.broadcasted_iota(jnp.int32, (rows, tq), 0), tq)
    kidx = lax.broadcasted_iota(jnp.int32, (rows, tq), 1)

    def body(j, carry):
        kb = k_ref[pl.ds(pl.multiple_of(j * tq, tq), tq), :]
        s = _dot_nt(qs_ref[...], kb)
        s = jnp.where(j * tq + kidx <= qpos, s, NEG_INF)
        m_prev = m_ref[...]
        m_new = jnp.maximum(m_prev, jnp.max(s, axis=-1, keepdims=True))
        corr = jnp.exp(m_prev - m_new)
        p = jnp.exp(s - m_new)
        l_ref[...] = l_ref[...] * corr + jnp.sum(p, axis=-1, keepdims=True)
        acc_ref[...] = acc_ref[...] * corr + _dot(p.astype(BF16), kb[:, :MLA_KV_LORA])
        m_ref[...] = m_new
        return carry

    lax.fori_loop(0, qi + 1, body, 0)
    for h in range(MLA_HEADS):
        lat = acc_ref[h * tq:(h + 1) * tq, :] / l_ref[h * tq:(h + 1) * tq, :]
        o_ref[:, h * MLA_V:(h + 1) * MLA_V] = _dot(lat.astype(BF16), wuv_ref[h]).astype(o_ref.dtype)


def _prompt_attn(q, kcat, wuv, *, nb, t):
    n = q.shape[0]
    tq = _tile(t, 128)
    nq = t // tq
    rows = MLA_HEADS * tq
    return pl.pallas_call(
        functools.partial(_prompt_attn_kernel, tq=tq),
        grid=(nb, nq),
        in_specs=[
            pl.BlockSpec((tq, MLA_HEADS * QK_PAD), lambda b, i: (b * nq + i, 0)),
            pl.BlockSpec((t, QK_PAD), lambda b, i: (b, 0)),
            pl.BlockSpec(wuv.shape, lambda b, i: (0, 0, 0)),
        ],
        out_specs=pl.BlockSpec((tq, MLA_HEADS * MLA_V), lambda b, i: (b * nq + i, 0)),
        out_shape=jax.ShapeDtypeStruct((n, MLA_HEADS * MLA_V), BF16),
        scratch_shapes=[
            pltpu.VMEM((rows, QK_PAD), BF16),
            pltpu.VMEM((rows, 1), F32),
            pltpu.VMEM((rows, 1), F32),
            pltpu.VMEM((rows, MLA_KV_LORA), F32),
        ],
        compiler_params=_cparams("arbitrary", "arbitrary"),
    )(q, kcat, wuv)


def _sample_attn_kernel(pt_ref, q_ref, ckvn_ref, krn_ref, wuv_ref, *rest, npg, nsteps, t):
    del pt_ref
    ckv_pages = rest[:npg]
    kr_pages = rest[npg:2 * npg]
    o_ref = rest[2 * npg]
    qs_ref, m_ref, l_ref, acc_ref, s_ref, kbf_ref, knc_ref, knr_ref = rest[2 * npg + 1:]
    j = pl.program_id(1)
    rows = MLA_HEADS * t
    page = ckv_pages[0].shape[0]

    @pl.when(j == 0)
    def _():
        for h in range(MLA_HEADS):
            qs_ref[h * t:(h + 1) * t, :] = q_ref[:, h * QK_PAD:(h + 1) * QK_PAD]
        m_ref[...] = jnp.full((rows, 1), NEG_INF, F32)
        l_ref[...] = jnp.zeros((rows, 1), F32)
        acc_ref[...] = jnp.zeros((rows, MLA_KV_LORA), F32)

    qb = qs_ref[...].astype(BF16)
    ql = qb[:, :MLA_KV_LORA]
    qr = qb[:, MLA_KV_LORA:MLA_KV_LORA + MLA_ROPE]

    def online_update(s, pv_fn):
        m_prev = m_ref[...]
        m_new = jnp.maximum(m_prev, jnp.max(s, axis=-1, keepdims=True))
        corr = jnp.exp(m_prev - m_new)
        p = jnp.exp(s - m_new)
        l_ref[...] = l_ref[...] * corr + jnp.sum(p, axis=-1, keepdims=True)
        acc_ref[...] = acc_ref[...] * corr + pv_fn(p.astype(BF16))
        m_ref[...] = m_new

    for k in range(npg):
        kc = ckv_pages[k][...].astype(BF16)
        kbf_ref[k] = kc
        s_ref[:, k * page:(k + 1) * page] = (
            _dot_nt(ql, kc) + _dot_nt(qr, kr_pages[k][...].astype(BF16)))

    def pv_pages(p):
        out = _dot(p[:, :page], kbf_ref[0])
        for k in range(1, npg):
            out = out + _dot(p[:, k * page:(k + 1) * page], kbf_ref[k])
        return out

    online_update(s_ref[...], pv_pages)

    @pl.when(j == nsteps - 1)
    def _():
        knc_ref[...] = jnp.zeros(knc_ref.shape, F32)
        knr_ref[...] = jnp.zeros(knr_ref.shape, F32)
        knc_ref[:t, :] = ckvn_ref[...]
        knr_ref[:t, :] = krn_ref[...]
        kc = knc_ref[...].astype(BF16)
        s = _dot_nt(ql, kc) + _dot_nt(qr, knr_ref[...].astype(BF16))
        tq = lax.rem(lax.broadcasted_iota(jnp.int32, (rows, page), 0), t)
        kidx = lax.broadcasted_iota(jnp.int32, (rows, page), 1)
        s = jnp.where(kidx <= tq, s, NEG_INF)
        online_update(s, lambda p: _dot(p, kc))
        lat = acc_ref[...] / l_ref[...]
        for h in range(MLA_HEADS):
            o_ref[:, h * MLA_V:(h + 1) * MLA_V] = _dot(
                lat[h * t:(h + 1) * t].astype(BF16), wuv_ref[h]).astype(o_ref.dtype)


def _sample_attn(q, ckv_new, kr_new, wuv, cache_ckv, cache_krope, page_table, *, layer, nb, t):
    n_pages = page_table.shape[1]
    page = cache_ckv.shape[2]
    npg = min(n_pages, 32)
    assert n_pages % npg == 0 and t <= page
    nsteps = n_pages // npg
    rows = MLA_HEADS * t
    pt_flat = page_table.reshape(-1)

    def page_spec(width, k):
        return pl.BlockSpec(
            (None, None, page, width),
            lambda b, j, pt: (layer, pt[b * n_pages + j * npg + k], 0, 0))

    in_specs = [
        pl.BlockSpec((t, MLA_HEADS * QK_PAD), lambda b, j, pt: (b, 0)),
        pl.BlockSpec((t, MLA_KV_LORA), lambda b, j, pt: (b, 0)),
        pl.BlockSpec((t, MLA_ROPE), lambda b, j, pt: (b, 0)),
        pl.BlockSpec(wuv.shape, lambda b, j, pt: (0, 0, 0)),
    ]
    in_specs += [page_spec(MLA_KV_LORA, k) for k in range(npg)]
    in_specs += [page_spec(MLA_ROPE, k) for k in range(npg)]
    grid_spec = pltpu.PrefetchScalarGridSpec(
        num_scalar_prefetch=1,
        grid=(nb, nsteps),
        in_specs=in_specs,
        out_specs=pl.BlockSpec((t, MLA_HEADS * MLA_V), lambda b, j, pt: (b, 0)),
        scratch_shapes=[
            pltpu.VMEM((rows, QK_PAD), F32),
            pltpu.VMEM((rows, 1), F32),
            pltpu.VMEM((rows, 1), F32),
            pltpu.VMEM((rows, MLA_KV_LORA), F32),
            pltpu.VMEM((rows, npg * page), F32),
            pltpu.VMEM((npg, page, MLA_KV_LORA), BF16),
            pltpu.VMEM((page, MLA_KV_LORA), F32),
            pltpu.VMEM((page, MLA_ROPE), F32),
        ],
    )
    return pl.pallas_call(
        functools.partial(_sample_attn_kernel, npg=npg, nsteps=nsteps, t=t),
        grid_spec=grid_spec,
        out_shape=jax.ShapeDtypeStruct((nb * t, MLA_HEADS * MLA_V), F32),
        compiler_params=_cparams("arbitrary", "arbitrary"),
    )(pt_flat, q, ckv_new, kr_new, wuv, *([cache_ckv] * npg), *([cache_krope] * npg))


def _out_proj_kernel(ohg_ref, omla_ref, x_ref, w1_ref, w2_ref, g_ref, x1_ref, h2_ref):
    y = _dot(ohg_ref[...].astype(BF16), w1_ref[...]) + _dot(omla_ref[...].astype(BF16), w2_ref[...])
    x1 = x_ref[...] + y
    x1_ref[...] = x1
    h2_ref[...] = _rms_rows(x1, g_ref[...]).astype(BF16)


def _out_proj(ohg, omla, x, w1, w2, g):
    n, d = x.shape
    tm = _tile(n, 256)
    row = lambda i: (i, 0)
    fixed = lambda i: (0, 0)
    return pl.pallas_call(
        _out_proj_kernel,
        grid=(n // tm,),
        in_specs=[
            pl.BlockSpec((tm, ohg.shape[1]), row),
            pl.BlockSpec((tm, omla.shape[1]), row),
            pl.BlockSpec((tm, d), row),
            pl.BlockSpec(w1.shape, fixed),
            pl.BlockSpec(w2.shape, fixed),
            pl.BlockSpec((1, d), fixed),
        ],
        out_specs=[pl.BlockSpec((tm, d), row), pl.BlockSpec((tm, d), row)],
        out_shape=[jax.ShapeDtypeStruct((n, d), F32), jax.ShapeDtypeStruct((n, d), BF16)],
        compiler_params=_cparams("arbitrary"),
    )(ohg, omla, x, w1, w2, g)


def _ffn_kernel(*refs, tm, tiles_per_seq, t_sample):
    if t_sample:
        h_ref, x_ref, wa_ref, wg_ref, wd_ref, cw_ref, cb_ref, sp_ref, o_ref, tail_ref = refs
    else:
        h_ref, x_ref, wa_ref, wg_ref, wd_ref, cw_ref, cb_ref, o_ref, tail_ref, halo_ref = refs
    i = pl.program_id(0)
    c = pl.program_id(1)
    h = h_ref[...]
    a = _dot(h, wa_ref[...])
    gate = _dot(h, wg_ref[...])
    tc = a.shape[1]
    r1 = pltpu.roll(a, 1, 0)
    r2 = pltpu.roll(a, 2, 0)
    if t_sample:
        sp = sp_ref[...]
        tt = lax.rem(lax.broadcasted_iota(jnp.int32, (tm, tc), 0), t_sample)
        a1 = jnp.where(tt == 0, pltpu.roll(sp, tm - (t_sample - 1), 0), r1)
        a2 = jnp.where(tt < 2, pltpu.roll(sp, tm - (t_sample - 2), 0), r2)
        tail_ref[...] = a
    else:
        @pl.when(lax.rem(i, tiles_per_seq) == 0)
        def _():
            halo_ref[c] = jnp.zeros((SUBLANES, tc), F32)

        halo = halo_ref[c]
        row8 = lax.broadcasted_iota(jnp.int32, (SUBLANES, tc), 0)
        top1 = jnp.where(row8 < 1, pltpu.roll(halo, 1, 0), r1[:SUBLANES])
        top2 = jnp.where(row8 < 2, pltpu.roll(halo, 2, 0), r2[:SUBLANES])
        a1 = jnp.concatenate([top1, r1[SUBLANES:]], axis=0)
        a2 = jnp.concatenate([top2, r2[SUBLANES:]], axis=0)
        halo_ref[c] = a[tm - SUBLANES:]
        tail_ref[0] = a[tm - SUBLANES:]
    cw = cw_ref[...]
    conv = ((cb_ref[...] + cw[0:1] * a2) + cw[1:2] * a1) + cw[2:3] * a
    act = (conv * jax.nn.sigmoid(conv)) * gate
    y = _dot(act.astype(BF16), wd_ref[...])

    @pl.when(c == 0)
    def _():
        o_ref[...] = x_ref[...] + y

    @pl.when(c > 0)
    def _():
        o_ref[...] += y


def _ffn(h2, x1, w_up, w_down, conv_w, conv_b, sp, *, t, t_sample):
    n, d = x1.shape
    dff = w_down.shape[0]
    tc = 512
    nc = dff // tc
    assert dff % tc == 0
    if t_sample:
        tm = _tile(n, 512)
        assert tm % t_sample == 0
        tiles_per_seq = 1
        tail_shape = (n, dff)
        tail_spec = pl.BlockSpec((tm, tc), lambda i, c: (i, c))
    else:
        tm = _tile(t, 512)
        tiles_per_seq = t // tm
        tail_shape = (n // tm, SUBLANES, dff)
        tail_spec = pl.BlockSpec((1, SUBLANES, tc), lambda i, c: (i, 0, c))
    in_specs = [
        pl.BlockSpec((tm, d), lambda i, c: (i, 0)),
        pl.BlockSpec((tm, d), lambda i, c: (i, 0)),
        pl.BlockSpec((d, tc), lambda i, c: (0, c)),
        pl.BlockSpec((d, tc), lambda i, c: (0, nc + c)),
        pl.BlockSpec((tc, d), lambda i, c: (c, 0)),
        pl.BlockSpec((CONV_W, tc), lambda i, c: (0, c)),
        pl.BlockSpec((1, tc), lambda i, c: (0, c)),
    ]
    args = [h2, x1, w_up, w_up, w_down, conv_w, conv_b]
    scratch = []
    if t_sample:
        in_specs.append(pl.BlockSpec((tm, tc), lambda i, c: (i, c)))
        args.append(sp)
    else:
        scratch.append(pltpu.VMEM((nc, SUBLANES, tc), F32))
    return pl.pallas_call(
        functools.partial(_ffn_kernel, tm=tm, tiles_per_seq=tiles_per_seq, t_sample=t_sample),
        grid=(n // tm, nc),
        in_specs=in_specs,
        out_specs=[pl.BlockSpec((tm, d), lambda i, c: (i, 0)), tail_spec],
        out_shape=[jax.ShapeDtypeStruct((n, d), F32), jax.ShapeDtypeStruct(tail_shape, F32)],
        scratch_shapes=scratch,
        compiler_params=_cparams("arbitrary", "arbitrary"),
    )(*args)


def _ple_kernel(x_ref, p_ref, gn_ref, wg_ref, wp_ref, fn_ref, y_ref, *, final):
    x = x_ref[...]
    hn = _rms_rows(x, gn_ref[...]).astype(BF16)
    gate = jax.nn.sigmoid(_dot(hn, wg_ref[...]))
    x3 = x + gate * _dot(p_ref[...].astype(BF16), wp_ref[...])
    y_ref[...] = _rms_rows(x3, fn_ref[...]) if final else x3


def _ple(x2, p, gn, wg, wp, fn, *, final):
    n, d = x2.shape
    tm = _tile(n, 256)
    row = lambda i: (i, 0)
    fixed = lambda i: (0, 0)
    return pl.pallas_call(
        functools.partial(_ple_kernel, final=final),
        grid=(n // tm,),
        in_specs=[
            pl.BlockSpec((tm, d), row),
            pl.BlockSpec((tm, p.shape[1]), row),
            pl.BlockSpec((1, d), fixed),
            pl.BlockSpec(wg.shape, fixed),
            pl.BlockSpec(wp.shape, fixed),
            pl.BlockSpec((1, d), fixed),
        ],
        out_specs=pl.BlockSpec((tm, d), row),
        out_shape=jax.ShapeDtypeStruct((n, d), F32),
        compiler_params=_cparams("arbitrary"),
    )(x2, p, gn, wg, wp, fn)


def _rot_cols(w):
    half = MLA_ROPE // 2
    return jnp.concatenate([-w[..., half:], w[..., :half]], axis=-1)


def _rope_table(pos):
    half = MLA_ROPE // 2
    inv = 1.0 / (ROPE_THETA ** (jnp.arange(half, dtype=F32) / half))
    ang = pos.astype(F32)[:, None] * inv[None, :]
    c, s = jnp.cos(ang), jnp.sin(ang)
    return jnp.concatenate([c, c, s, s], axis=-1)


def _layer_weights(l, norm_mix, w_in, hg_onorm, mla_q_norm, w_q_b, mla_kv_norm, w_kv_b, w_out,
                   norm_ffn, w_up, conv_w, conv_b, w_down, norm_ple, w_ple_gate, w_ple_proj):
    d = w_in.shape[1]
    wi = w_in[l]
    kr_cols = wi[:, HG_COLS + MLA_Q_LORA + MLA_KV_LORA:]
    pad = MLA_COLS - (MLA_Q_LORA + MLA_KV_LORA + 2 * MLA_ROPE)
    w_in_ext = jnp.concatenate([wi, _rot_cols(kr_cols), jnp.zeros((d, pad), wi.dtype)],
                               axis=1).astype(BF16)
    wq = w_q_b[l].reshape(MLA_Q_LORA, MLA_HEADS, MLA_NOPE + MLA_ROPE)
    wq_ext = jnp.concatenate([wq, _rot_cols(wq[..., MLA_NOPE:])], axis=-1)
    wq_ext = wq_ext.reshape(MLA_Q_LORA, MLA_HEADS * Q_HEAD_COLS).astype(BF16)
    wkv = w_kv_b[l].reshape(MLA_KV_LORA, MLA_HEADS, MLA_NOPE + MLA_V)
    wuk = jnp.transpose(wkv[..., :MLA_NOPE], (1, 2, 0)).astype(BF16)
    wuv = jnp.transpose(wkv[..., MLA_NOPE:], (1, 0, 2)).astype(BF16)
    wo = w_out[l].astype(BF16)
    hgw = HG_HEADS * HG_DV
    return dict(
        norm_mix=norm_mix[l][None], w_in=w_in_ext, hg_onorm=hg_onorm[l][None],
        q_norm=mla_q_norm[l][None], wq=wq_ext, kv_norm=mla_kv_norm[l][None], wuk=wuk, wuv=wuv,
        wo_hg=wo[:hgw], wo_mla=wo[hgw:], norm_ffn=norm_ffn[l][None], w_up=w_up[l].astype(BF16),
        conv_w=conv_w[l], conv_b=conv_b[l][None], w_down=w_down[l].astype(BF16),
        norm_ple=norm_ple[l][None], w_ple_gate=w_ple_gate[l].astype(BF16),
        w_ple_proj=w_ple_proj[l].astype(BF16))


def _trunk(x, p_l, w, hg_lower, norm_final, cs, *, layer, nb, t, final, hg_s0, attend, sample):
    proj = _in_proj(x, w["norm_mix"], w["w_in"])
    ckv, kr, kcat, q = _mla_proj(proj, cs, w["q_norm"], w["kv_norm"], w["wq"], w["wuk"],
                                 F32 if sample else BF16)
    o_hg, hg_s = _hgrn(proj, hg_lower, w["hg_onorm"], hg_s0, layer=layer, nb=nb, t=t,
                       seq_per_step=min(nb, 16) if sample else 1,
                       out_dtype=F32 if sample else BF16)
    o_mla = attend(q, ckv, kr, kcat, w["wuv"])
    x1, h2 = _out_proj(o_hg, o_mla, x, w["wo_hg"], w["wo_mla"], w["norm_ffn"])
    return x1, h2, ckv, kr, hg_s


def kernel(x_prompt, x_sample, cache_ckv, cache_krope, state_hgrn, state_conv, page_table,
           p_prompt, p_sample, norm_mix, w_in, hg_lower, hg_onorm, mla_q_norm, w_q_b,
           mla_kv_norm, w_kv_b, w_out, norm_ffn, w_up, conv_w, conv_b, w_down, norm_ple,
           w_ple_gate, w_ple_proj, norm_final):
    depth = w_in.shape[0]
    bp, tp, d = x_prompt.shape
    bs, ts, _ = x_sample.shape
    dff = w_down.shape[1]
    cs_p = _rope_table(jnp.arange(tp, dtype=jnp.int32))
    cs_s = jnp.tile(_rope_table(PAST_LEN + jnp.arange(ts, dtype=jnp.int32)), (bs, 1))
    nf = norm_final[None]
    xp = x_prompt.reshape(bp * tp, d)
    xs = x_sample.reshape(bs * ts, d)
    outs_p, outs_s = [], []
    for l in range(depth):
        w = _layer_weights(l, norm_mix, w_in, hg_onorm, mla_q_norm, w_q_b, mla_kv_norm, w_kv_b,
                           w_out, norm_ffn, w_up, conv_w, conv_b, w_down, norm_ple, w_ple_gate,
                           w_ple_proj)
        final = l == depth - 1

        attend_p = lambda q, ckv, kr, kcat, wuv: _prompt_attn(q, kcat, wuv, nb=bp, t=tp)
        x1, h2, ckv_p, kr_p, hg_p = _trunk(xp, None, w, hg_lower, nf, cs_p, layer=l, nb=bp, t=tp,
                                           final=final, hg_s0=None, attend=attend_p, sample=False)
        x2, tail = _ffn(h2, x1, w["w_up"], w["w_down"], w["conv_w"], w["conv_b"], None,
                        t=tp, t_sample=0)
        cv_p = tail.reshape(bp, -1, SUBLANES, dff)[:, -1, SUBLANES - (CONV_W - 1):]
        xp = _ple(x2, p_prompt[l].reshape(bp * tp, -1), w["norm_ple"], w["w_ple_gate"],
                  w["w_ple_proj"], nf, final=final)
        outs_p.append((ckv_p.reshape(bp, tp, -1), kr_p.reshape(bp, tp, -1), hg_p, cv_p))

        attend_s = lambda q, ckv, kr, kcat, wuv: _sample_attn(
            q, ckv, kr, wuv, cache_ckv, cache_krope, page_table, layer=l, nb=bs, t=ts)
        x1, h2, ckv_s, kr_s, hg_s = _trunk(xs, None, w, hg_lower, nf, cs_s, layer=l, nb=bs, t=ts,
                                           final=final, hg_s0=state_hgrn[l], attend=attend_s,
                                           sample=True)
        sp = jnp.pad(state_conv[l], ((0, 0), (ts - (CONV_W - 1), 0), (0, 0))).reshape(bs * ts, dff)
        x2, a_full = _ffn(h2, x1, w["w_up"], w["w_down"], w["conv_w"], w["conv_b"], sp,
                          t=ts, t_sample=ts)
        cv_s = a_full.reshape(bs, ts, dff)[:, ts - (CONV_W - 1):]
        xs = _ple(x2, p_sample[l].reshape(bs * ts, -1), w["norm_ple"], w["w_ple_gate"],
                  w["w_ple_proj"], nf, final=final)
        outs_s.append((ckv_s.reshape(bs, ts, -1), kr_s.reshape(bs, ts, -1), hg_s, cv_s))

    stack = lambda outs, k: jnp.stack([o[k] for o in outs])
    return (xp.reshape(bp, tp, d), xs.reshape(bs, ts, d),
            stack(outs_p, 0), stack(outs_p, 1), stack(outs_p, 2), stack(outs_p, 3),
            stack(outs_s, 0), stack(outs_s, 1), stack(outs_s, 2), stack(outs_s, 3))
```

```python
import functools

import jax
import jax.numpy as jnp
from jax import lax
from jax.experimental import pallas as pl
from jax.experimental.pallas import tpu as pltpu

F32 = jnp.float32
BF16 = jnp.bfloat16

HG_HEADS = 8
HG_DK = 128
HG_DV = 128
HG_CHUNK = 64
HG_GROUP_ROWS = 256
ATTN_ROW_GROUPS = 2
MLA_HEADS = 8
MLA_Q_LORA = 512
MLA_KV_LORA = 256
MLA_NOPE = 128
MLA_ROPE = 64
MLA_V = 128
MLA_SCALE = (MLA_NOPE + MLA_ROPE) ** -0.5
ROPE_THETA = 10000.0
PAST_LEN = 16384
CONV_W = 3
EPS = 1e-6
NEG_INF = -1e30

HG_COLS = 4 * HG_HEADS * HG_DK
MLA_COLS = 1024
Q_HEAD_COLS = 256
QK_PAD = 384
LANES = 128
SUBLANES = 8

VMEM_LIMIT = 48 * 1024 * 1024

_NT = (((1,), (1,)), ((), ()))
_TN = (((0,), (0,)), ((), ()))


def _cparams(*sem):
    return pltpu.CompilerParams(dimension_semantics=sem, vmem_limit_bytes=VMEM_LIMIT)


def _tile(n, pref):
    t = min(n, pref)
    assert n % t == 0, (n, pref)
    return t


def _rms_rows(x, g):
    ms = jnp.mean(x * x, axis=-1, keepdims=True)
    return (x * lax.rsqrt(ms + EPS)) * g


def _dot(a, b):
    return jnp.dot(a, b, preferred_element_type=F32)


def _dot_nt(a, b):
    return lax.dot_general(a, b, _NT, preferred_element_type=F32)


def _in_proj_kernel(x_ref, g_ref, w_ref, o_ref, h_ref):
    @pl.when(pl.program_id(1) == 0)
    def _():
        h_ref[...] = _rms_rows(x_ref[...], g_ref[...]).astype(BF16)

    o_ref[...] = _dot(h_ref[...], w_ref[...])


def _in_proj(x, g, w):
    n, d = x.shape
    cols = w.shape[1]
    tm = _tile(n, 512)
    tn = 1024
    return pl.pallas_call(
        _in_proj_kernel,
        name="in_proj",
        grid=(n // tm, cols // tn),
        in_specs=[
            pl.BlockSpec((tm, d), lambda i, j: (i, 0)),
            pl.BlockSpec((1, d), lambda i, j: (0, 0)),
            pl.BlockSpec((d, tn), lambda i, j: (0, j)),
        ],
        out_specs=pl.BlockSpec((tm, tn), lambda i, j: (i, j)),
        out_shape=jax.ShapeDtypeStruct((n, cols), F32),
        scratch_shapes=[pltpu.VMEM((tm, d), BF16)],
        compiler_params=_cparams("arbitrary", "arbitrary"),
    )(x, g, w)


def _mla_proj_kernel(p_ref, cs_ref, qn_ref, kvn_ref, wq_ref, wuk_ref,
                     ckv_ref, kr_ref, kcat_ref, q_ref):
    p = p_ref[...]
    cs = cs_ref[...]
    tm = p.shape[0]
    lane = lax.broadcasted_iota(jnp.int32, (tm, LANES), 1)
    cq = _rms_rows(p[:, :MLA_Q_LORA], qn_ref[...]).astype(BF16)
    ckv = _rms_rows(p[:, MLA_Q_LORA:MLA_Q_LORA + MLA_KV_LORA], kvn_ref[...])
    ckv_ref[...] = ckv
    kk = p[:, 768:896] * cs
    kro = kk + pltpu.roll(kk, MLA_ROPE, 1)
    kr_ref[...] = kro[:, :MLA_ROPE]
    kcat_ref[:, :MLA_KV_LORA] = ckv.astype(kcat_ref.dtype)
    kcat_ref[:, MLA_KV_LORA:] = jnp.where(lane < MLA_ROPE, kro, 0.0).astype(kcat_ref.dtype)
    q = _dot(cq, wq_ref[...])
    for h in range(MLA_HEADS):
        qh = q[:, h * Q_HEAD_COLS:(h + 1) * Q_HEAD_COLS]
        qlat = _dot(qh[:, :MLA_NOPE].astype(BF16), wuk_ref[h]) * MLA_SCALE
        rr = qh[:, MLA_NOPE:] * cs
        rope = (rr + pltpu.roll(rr, MLA_ROPE, 1)) * MLA_SCALE
        q_ref[:, h * QK_PAD:h * QK_PAD + MLA_KV_LORA] = qlat.astype(q_ref.dtype)
        q_ref[:, h * QK_PAD + MLA_KV_LORA:(h + 1) * QK_PAD] = (
            jnp.where(lane < MLA_ROPE, rope, 0.0).astype(q_ref.dtype))


def _mla_proj(proj, cs, qn, kvn, wq, wuk, q_dtype):
    n = proj.shape[0]
    tm = _tile(n, 256)
    cs_blocks = cs.shape[0] // tm
    mla_blk = HG_COLS // MLA_COLS
    return pl.pallas_call(
        _mla_proj_kernel,
        name="mla_proj",
        grid=(n // tm,),
        in_specs=[
            pl.BlockSpec((tm, MLA_COLS), lambda i: (i, mla_blk)),
            pl.BlockSpec((tm, LANES), lambda i: (i % cs_blocks, 0)),
            pl.BlockSpec((1, MLA_Q_LORA), lambda i: (0, 0)),
            pl.BlockSpec((1, MLA_KV_LORA), lambda i: (0, 0)),
            pl.BlockSpec(wq.shape, lambda i: (0, 0)),
            pl.BlockSpec(wuk.shape, lambda i: (0, 0, 0)),
        ],
        out_specs=[
            pl.BlockSpec((tm, MLA_KV_LORA), lambda i: (i, 0)),
            pl.BlockSpec((tm, MLA_ROPE), lambda i: (i, 0)),
            pl.BlockSpec((tm, QK_PAD), lambda i: (i, 0)),
            pl.BlockSpec((tm, MLA_HEADS * QK_PAD), lambda i: (i, 0)),
        ],
        out_shape=[
            jax.ShapeDtypeStruct((n, MLA_KV_LORA), F32),
            jax.ShapeDtypeStruct((n, MLA_ROPE), F32),
            jax.ShapeDtypeStruct((n, QK_PAD), BF16),
            jax.ShapeDtypeStruct((n, MLA_HEADS * QK_PAD), q_dtype),
        ],
        compiler_params=_cparams("arbitrary"),
    )(proj, cs, qn, kvn, wq, wuk)


def _hgrn_kernel(*refs, layer, nseq, nchunk, chunk, has_s0):
    if has_s0:
        hl_ref, q_ref, f_ref, i_ref, g_ref, on_ref, s0_ref, o_ref, sout_ref = refs
    else:
        hl_ref, q_ref, f_ref, i_ref, g_ref, on_ref, o_ref, sout_ref = refs
        s0_ref = None
    hl = hl_ref[...]
    e = jnp.exp(hl - jnp.max(hl, axis=0, keepdims=True))
    lb = jnp.sum(e[:layer + 1], axis=0, keepdims=True) / jnp.sum(e, axis=0, keepdims=True)
    onorm = on_ref[...]
    sup = min(HG_GROUP_ROWS // chunk, nseq * nchunk)
    rg = sup * chunk
    ngroups = (nseq * nchunk) // sup
    assert (nseq * nchunk) % sup == 0 and (nseq == 1 or nchunk == 1)
    rix = lax.broadcasted_iota(jnp.int32, (rg, rg), 0)
    cix = lax.broadcasted_iota(jnp.int32, (rg, rg), 1)
    same = (rix // chunk) == (cix // chunk)
    causal = jnp.logical_and(same, rix >= cix)
    scan_lhs = jnp.concatenate([causal, same], axis=0).astype(BF16)

    def group_body(s, st):
        r0 = pl.multiple_of(s * rg, rg)
        qr = q_ref[pl.ds(r0, rg), :]
        fx = f_ref[pl.ds(r0, rg), :]
        v = i_ref[pl.ds(r0, rg), :]
        gr = g_ref[pl.ds(r0, rg), :]
        sg = jax.nn.sigmoid(fx)
        f = lb + (1.0 - lb) * sg
        k = (1.0 - lb) * (1.0 - sg)
        logf = jnp.log(f)
        hi = logf.astype(BF16)
        r1 = logf - hi.astype(F32)
        mid = r1.astype(BF16)
        lo = (r1 - mid.astype(F32)).astype(BF16)
        parts = _dot(scan_lhs, jnp.concatenate([hi, mid, lo], axis=1))
        sums = (parts[:, :HG_DK] + parts[:, HG_DK:2 * HG_DK]) + parts[:, 2 * HG_DK:]
        cum = sums[:rg]
        tot = sums[rg:]
        qd = (qr * jax.nn.sigmoid(qr)) * jnp.exp(cum)
        ki = (k * jnp.exp(-cum)).astype(BF16)
        ke = k * jnp.exp(tot - cum)
        dec = jnp.exp(tot)
        a = jnp.where(causal, _dot_nt(qd.astype(BF16), ki), 0.0)
        o_intra = _dot(a.astype(BF16), v.astype(BF16))
        o_state = []
        for c in range(sup):
            cs = slice(c * chunk, (c + 1) * chunk)
            if nchunk == 1:
                g = s * sup + c
                st = s0_ref[g, 0].T if has_s0 else jnp.zeros((HG_DV, HG_DK), F32)
            qd_c = qd[cs].astype(BF16)
            o_state.append(_dot_nt(qd_c, st.astype(BF16)))
            upd = lax.dot_general(v[cs].astype(BF16), ke[cs].astype(BF16), _TN,
                                  preferred_element_type=F32)
            st = st * dec[c * chunk:c * chunk + 1] + upd
            if nchunk == 1:
                sout_ref[g, 0] = st.T
        o = o_intra + jnp.concatenate(o_state, axis=0)
        out = _rms_rows(o, onorm) * jax.nn.sigmoid(gr)
        o_ref[pl.ds(r0, rg), :] = out.astype(o_ref.dtype)
        return st

    if nchunk == 1:
        lax.fori_loop(0, ngroups, group_body, jnp.zeros((HG_DV, HG_DK), F32))
    else:
        st0 = s0_ref[0, 0].T if has_s0 else jnp.zeros((HG_DV, HG_DK), F32)
        sout_ref[0, 0] = lax.fori_loop(0, ngroups, group_body, st0).T


def _hgrn(proj, hg_lower, onorm, s0, *, layer, nb, t, seq_per_step, out_dtype):
    n = proj.shape[0]
    assert n == nb * t
    chunk = min(t, HG_CHUNK)
    assert t % chunk == 0 and nb % seq_per_step == 0
    rows = seq_per_step * t
    nl = hg_lower.shape[0]

    def col(seg):
        return pl.BlockSpec((rows, HG_DK), lambda i, h: (i, seg * HG_HEADS + h))

    in_specs = [
        pl.BlockSpec((nl, HG_DK), lambda i, h: (0, h)),
        col(0), col(1), col(2), col(3),
        pl.BlockSpec((1, HG_DV), lambda i, h: (0, h)),
    ]
    args = [hg_lower, proj, proj, proj, proj, onorm]
    state_spec = pl.BlockSpec((seq_per_step, 1, HG_DK, HG_DV), lambda i, h: (i, h, 0, 0))
    if s0 is not None:
        in_specs.append(state_spec)
        args.append(s0)
    kern = functools.partial(_hgrn_kernel, layer=layer, nseq=seq_per_step, nchunk=t // chunk,
                             chunk=chunk, has_s0=s0 is not None)
    return pl.pallas_call(
        kern,
        name="hgrn",
        grid=(nb // seq_per_step, HG_HEADS),
        in_specs=in_specs,
        out_specs=[
            pl.BlockSpec((rows, HG_DV), lambda i, h: (i, h)),
            state_spec,
        ],
        out_shape=[
            jax.ShapeDtypeStruct((n, HG_HEADS * HG_DV), out_dtype),
            jax.ShapeDtypeStruct((nb, HG_HEADS, HG_DK, HG_DV), F32),
        ],
        compiler_params=_cparams("arbitrary", "arbitrary"),
    )(*args)


def _prompt_attn_kernel(q_ref, k_ref, wuv_ref, o_ref, qs_ref, m_ref, l_ref, acc_ref, *, tq, tk):
    qi = pl.program_id(1)
    rows = MLA_HEADS * tq
    for h in range(MLA_HEADS):
        qs_ref[h * tq:(h + 1) * tq, :] = q_ref[:, h * QK_PAD:(h + 1) * QK_PAD]
    m_ref[...] = jnp.full((rows, 1), NEG_INF, F32)
    l_ref[...] = jnp.zeros((rows, 1), F32)
    acc_ref[...] = jnp.zeros((rows, MLA_KV_LORA), F32)
    nfull = (qi * tq) // tk
    rg = rows // ATTN_ROW_GROUPS

    def lane_fold(x, op):
        out = x[:, :LANES]
        for i in range(1, x.shape[1] // LANES):
            out = op(out, x[:, i * LANES:(i + 1) * LANES])
        return out

    def block(j, masked):
        kb = k_ref[pl.ds(pl.multiple_of(j * tk, tk), tk), :]
        for g in range(ATTN_ROW_GROUPS):
            sl = slice(g * rg, (g + 1) * rg)
            s = _dot_nt(qs_ref[sl, :], kb)
            if masked:
                tok = lax.rem(lax.broadcasted_iota(jnp.int32, (rg, tk), 0), tq)
                kidx = lax.broadcasted_iota(jnp.int32, (rg, tk), 1)
                s = jnp.where(kidx - tok <= qi * tq - j * tk, s, NEG_INF)
            m_prev = m_ref[sl, :]
            m_new = jnp.maximum(m_prev, jnp.max(lane_fold(s, jnp.maximum), axis=-1, keepdims=True))
            corr = jnp.exp(m_prev - m_new)
            p = jnp.exp(s - m_new)
            l_ref[sl, :] = l_ref[sl, :] * corr + jnp.sum(lane_fold(p, jnp.add), axis=-1,
                                                        keepdims=True)
            acc_ref[sl, :] = acc_ref[sl, :] * corr + _dot(p.astype(BF16), kb[:, :MLA_KV_LORA])
            m_ref[sl, :] = m_new

    def body(j, carry):
        block(j, False)
        return carry

    lax.fori_loop(0, nfull, body, 0)
    block(nfull, True)
    for h in range(MLA_HEADS):
        lat = acc_ref[h * tq:(h + 1) * tq, :] / l_ref[h * tq:(h + 1) * tq, :]
        o_ref[:, h * MLA_V:(h + 1) * MLA_V] = _dot(lat.astype(BF16), wuv_ref[h]).astype(o_ref.dtype)


def _prompt_attn(q, kcat, wuv, *, nb, t):
    n = q.shape[0]
    tq = _tile(t, 128)
    tk = _tile(t, 512)
    assert tk % tq == 0
    nq = t // tq
    rows = MLA_HEADS * tq
    return pl.pallas_call(
        functools.partial(_prompt_attn_kernel, tq=tq, tk=tk),
        name="prompt_attn",
        grid=(nb, nq),
        in_specs=[
            pl.BlockSpec((tq, MLA_HEADS * QK_PAD), lambda b, i: (b * nq + i, 0)),
            pl.BlockSpec((t, QK_PAD), lambda b, i: (b, 0)),
            pl.BlockSpec(wuv.shape, lambda b, i: (0, 0, 0)),
        ],
        out_specs=pl.BlockSpec((tq, MLA_HEADS * MLA_V), lambda b, i: (b * nq + i, 0)),
        out_shape=jax.ShapeDtypeStruct((n, MLA_HEADS * MLA_V), BF16),
        scratch_shapes=[
            pltpu.VMEM((rows, QK_PAD), BF16),
            pltpu.VMEM((rows, 1), F32),
            pltpu.VMEM((rows, 1), F32),
            pltpu.VMEM((rows, MLA_KV_LORA), F32),
        ],
        compiler_params=_cparams("arbitrary", "arbitrary"),
    )(q, kcat, wuv)


def _sample_attn_kernel(pt_ref, q_ref, ckvn_ref, krn_ref, wuv_ref, *rest, npg, nsteps, t):
    del pt_ref
    ckv_pages = rest[:npg]
    kr_pages = rest[npg:2 * npg]
    o_ref = rest[2 * npg]
    qs_ref, m_ref, l_ref, acc_ref, kc_ref, krt_ref, knc_ref, knr_ref = rest[2 * npg + 1:]
    j = pl.program_id(1)
    rows = MLA_HEADS * t
    page = ckv_pages[0].shape[0]

    @pl.when(j == 0)
    def _():
        for h in range(MLA_HEADS):
            qs_ref[h * t:(h + 1) * t, :] = q_ref[:, h * QK_PAD:(h + 1) * QK_PAD]
        m_ref[...] = jnp.full((rows, 1), NEG_INF, F32)
        l_ref[...] = jnp.zeros((rows, 1), F32)
        acc_ref[...] = jnp.zeros((rows, MLA_KV_LORA), F32)

    qb = qs_ref[...].astype(BF16)
    ql = qb[:, :MLA_KV_LORA]
    qr = qb[:, MLA_KV_LORA:MLA_KV_LORA + MLA_ROPE]

    def online_update(s, pv_fn):
        m_prev = m_ref[...]
        m_new = jnp.maximum(m_prev, jnp.max(s, axis=-1, keepdims=True))
        corr = jnp.exp(m_prev - m_new)
        p = jnp.exp(s - m_new)
        l_ref[...] = l_ref[...] * corr + jnp.sum(p, axis=-1, keepdims=True)
        acc_ref[...] = acc_ref[...] * corr + pv_fn(p.astype(BF16))
        m_ref[...] = m_new

    for k in range(npg):
        kc_ref[k * page:(k + 1) * page, :] = ckv_pages[k][...].astype(BF16)
        krt_ref[:, k * page:(k + 1) * page] = kr_pages[k][...].astype(BF16)
    kc_all = kc_ref[...]
    online_update(_dot_nt(ql, kc_all) + _dot(qr, krt_ref[...]), lambda p: _dot(p, kc_all))

    @pl.when(j == nsteps - 1)
    def _():
        knc_ref[...] = jnp.zeros(knc_ref.shape, F32)
        knr_ref[...] = jnp.zeros(knr_ref.shape, F32)
        knc_ref[:t, :] = ckvn_ref[...]
        knr_ref[:t, :] = krn_ref[...]
        kc = knc_ref[...].astype(BF16)
        s = _dot_nt(ql, kc) + _dot_nt(qr, knr_ref[...].astype(BF16))
        tq = lax.rem(lax.broadcasted_iota(jnp.int32, (rows, page), 0), t)
        kidx = lax.broadcasted_iota(jnp.int32, (rows, page), 1)
        s = jnp.where(kidx <= tq, s, NEG_INF)
        online_update(s, lambda p: _dot(p, kc))
        lat = acc_ref[...] / l_ref[...]
        for h in range(MLA_HEADS):
            o_ref[:, h * MLA_V:(h + 1) * MLA_V] = _dot(
                lat[h * t:(h + 1) * t].astype(BF16), wuv_ref[h]).astype(o_ref.dtype)


def _sample_attn(q, ckv_new, kr_new, wuv, cache_ckv, cache_krope, page_table, *, layer, nb, t):
    n_pages = page_table.shape[1]
    page = cache_ckv.shape[2]
    npg = min(n_pages, 32)
    assert n_pages % npg == 0 and t <= page
    nsteps = n_pages // npg
    rows = MLA_HEADS * t
    pt_flat = page_table.reshape(-1)
    krope_t = jnp.swapaxes(cache_krope, 2, 3)

    def page_spec(shape, k):
        return pl.BlockSpec(
            (None, None) + shape,
            lambda b, j, pt: (layer, pt[b * n_pages + j * npg + k], 0, 0))

    in_specs = [
        pl.BlockSpec((t, MLA_HEADS * QK_PAD), lambda b, j, pt: (b, 0)),
        pl.BlockSpec((t, MLA_KV_LORA), lambda b, j, pt: (b, 0)),
        pl.BlockSpec((t, MLA_ROPE), lambda b, j, pt: (b, 0)),
        pl.BlockSpec(wuv.shape, lambda b, j, pt: (0, 0, 0)),
    ]
    in_specs += [page_spec((page, MLA_KV_LORA), k) for k in range(npg)]
    in_specs += [page_spec((MLA_ROPE, page), k) for k in range(npg)]
    grid_spec = pltpu.PrefetchScalarGridSpec(
        num_scalar_prefetch=1,
        grid=(nb, nsteps),
        in_specs=in_specs,
        out_specs=pl.BlockSpec((t, MLA_HEADS * MLA_V), lambda b, j, pt: (b, 0)),
        scratch_shapes=[
            pltpu.VMEM((rows, QK_PAD), F32),
            pltpu.VMEM((rows, 1), F32),
            pltpu.VMEM((rows, 1), F32),
            pltpu.VMEM((rows, MLA_KV_LORA), F32),
            pltpu.VMEM((npg * page, MLA_KV_LORA), BF16),
            pltpu.VMEM((MLA_ROPE, npg * page), BF16),
            pltpu.VMEM((page, MLA_KV_LORA), F32),
            pltpu.VMEM((page, MLA_ROPE), F32),
        ],
    )
    return pl.pallas_call(
        functools.partial(_sample_attn_kernel, npg=npg, nsteps=nsteps, t=t),
        name="sample_attn",
        grid_spec=grid_spec,
        out_shape=jax.ShapeDtypeStruct((nb * t, MLA_HEADS * MLA_V), F32),
        compiler_params=_cparams("arbitrary", "arbitrary"),
    )(pt_flat, q, ckv_new, kr_new, wuv, *([cache_ckv] * npg), *([krope_t] * npg))


def _out_proj_kernel(ohg_ref, omla_ref, x_ref, w1_ref, w2_ref, g_ref, x1_ref, h2_ref):
    y = _dot(ohg_ref[...].astype(BF16), w1_ref[...]) + _dot(omla_ref[...].astype(BF16), w2_ref[...])
    x1 = x_ref[...] + y
    x1_ref[...] = x1
    h2_ref[...] = _rms_rows(x1, g_ref[...]).astype(BF16)


def _out_proj(ohg, omla, x, w1, w2, g):
    n, d = x.shape
    tm = _tile(n, 256)
    row = lambda i: (i, 0)
    fixed = lambda i: (0, 0)
    return pl.pallas_call(
        _out_proj_kernel,
        name="out_proj",
        grid=(n // tm,),
        in_specs=[
            pl.BlockSpec((tm, ohg.shape[1]), row),
            pl.BlockSpec((tm, omla.shape[1]), row),
            pl.BlockSpec((tm, d), row),
            pl.BlockSpec(w1.shape, fixed),
            pl.BlockSpec(w2.shape, fixed),
            pl.BlockSpec((1, d), fixed),
        ],
        out_specs=[pl.BlockSpec((tm, d), row), pl.BlockSpec((tm, d), row)],
        out_shape=[jax.ShapeDtypeStruct((n, d), F32), jax.ShapeDtypeStruct((n, d), BF16)],
        compiler_params=_cparams("arbitrary"),
    )(ohg, omla, x, w1, w2, g)


def _ffn_kernel(*refs, tm, tiles_per_seq, t_sample):
    if t_sample:
        h_ref, x_ref, wa_ref, wg_ref, wd_ref, cw_ref, cb_ref, sp_ref, o_ref, tail_ref = refs
    else:
        h_ref, x_ref, wa_ref, wg_ref, wd_ref, cw_ref, cb_ref, o_ref, tail_ref, halo_ref = refs
    i = pl.program_id(0)
    c = pl.program_id(1)

    @pl.when(c == 0)
    def _():
        o_ref[...] = x_ref[...]

    h = h_ref[...]
    a = _dot(h, wa_ref[...])
    gate = _dot(h, wg_ref[...])
    tc = a.shape[1]
    r1 = pltpu.roll(a, 1, 0)
    r2 = pltpu.roll(a, 2, 0)
    if t_sample:
        sp = sp_ref[...]
        tt = lax.rem(lax.broadcasted_iota(jnp.int32, (tm, tc), 0), t_sample)
        a1 = jnp.where(tt == 0, pltpu.roll(sp, tm - (t_sample - 1), 0), r1)
        a2 = jnp.where(tt < 2, pltpu.roll(sp, tm - (t_sample - 2), 0), r2)
        tail_ref[...] = a
    else:
        @pl.when(lax.rem(i, tiles_per_seq) == 0)
        def _():
            halo_ref[c] = jnp.zeros((SUBLANES, tc), F32)

        halo = halo_ref[c]
        row8 = lax.broadcasted_iota(jnp.int32, (SUBLANES, tc), 0)
        top1 = jnp.where(row8 < 1, pltpu.roll(halo, 1, 0), r1[:SUBLANES])
        top2 = jnp.where(row8 < 2, pltpu.roll(halo, 2, 0), r2[:SUBLANES])
        a1 = jnp.concatenate([top1, r1[SUBLANES:]], axis=0)
        a2 = jnp.concatenate([top2, r2[SUBLANES:]], axis=0)
        halo_ref[c] = a[tm - SUBLANES:]
        tail_ref[0] = a[tm - SUBLANES:]
    cw = cw_ref[...]
    conv = ((cb_ref[...] + cw[0:1] * a2) + cw[1:2] * a1) + cw[2:3] * a
    act = (conv * jax.nn.sigmoid(conv)) * gate
    o_ref[...] += _dot(act.astype(BF16), wd_ref[...])


def _ffn(h2, x1, w_up, w_down, conv_w, conv_b, sp, *, t, t_sample):
    n, d = x1.shape
    dff = w_down.shape[0]
    tc = 512
    nc = dff // tc
    assert dff % tc == 0
    if t_sample:
        tm = _tile(n, 512)
        assert tm % t_sample == 0
        tiles_per_seq = 1
        tail_shape = (n, dff)
        tail_spec = pl.BlockSpec((tm, tc), lambda i, c: (i, c))
    else:
        tm = _tile(t, 512)
        tiles_per_seq = t // tm
        tail_shape = (n // tm, SUBLANES, dff)
        tail_spec = pl.BlockSpec((1, SUBLANES, tc), lambda i, c: (i, 0, c))
    in_specs = [
        pl.BlockSpec((tm, d), lambda i, c: (i, 0)),
        pl.BlockSpec((tm, d), lambda i, c: (i, 0)),
        pl.BlockSpec((d, tc), lambda i, c: (0, c)),
        pl.BlockSpec((d, tc), lambda i, c: (0, nc + c)),
        pl.BlockSpec((tc, d), lambda i, c: (c, 0)),
        pl.BlockSpec((CONV_W, tc), lambda i, c: (0, c)),
        pl.BlockSpec((1, tc), lambda i, c: (0, c)),
    ]
    args = [h2, x1, w_up, w_up, w_down, conv_w, conv_b]
    scratch = []
    if t_sample:
        in_specs.append(pl.BlockSpec((tm, tc), lambda i, c: (i, c)))
        args.append(sp)
    else:
        scratch.append(pltpu.VMEM((nc, SUBLANES, tc), F32))
    return pl.pallas_call(
        functools.partial(_ffn_kernel, tm=tm, tiles_per_seq=tiles_per_seq, t_sample=t_sample),
        name="ffn",
        grid=(n // tm, nc),
        in_specs=in_specs,
        out_specs=[pl.BlockSpec((tm, d), lambda i, c: (i, 0)), tail_spec],
        out_shape=[jax.ShapeDtypeStruct((n, d), F32), jax.ShapeDtypeStruct(tail_shape, F32)],
        scratch_shapes=scratch,
        compiler_params=_cparams("arbitrary", "arbitrary"),
    )(*args)


def _ple_kernel(x_ref, p_ref, gn_ref, wg_ref, wp_ref, fn_ref, y_ref, *, final):
    x = x_ref[...]
    hn = _rms_rows(x, gn_ref[...]).astype(BF16)
    gate = jax.nn.sigmoid(_dot(hn, wg_ref[...]))
    x3 = x + gate * _dot(p_ref[...].astype(BF16), wp_ref[...])
    y_ref[...] = _rms_rows(x3, fn_ref[...]) if final else x3


def _ple(x2, p, gn, wg, wp, fn, *, final):
    n, d = x2.shape
    tm = _tile(n, 256)
    row = lambda i: (i, 0)
    fixed = lambda i: (0, 0)
    return pl.pallas_call(
        functools.partial(_ple_kernel, final=final),
        name="ple",
        grid=(n // tm,),
        in_specs=[
            pl.BlockSpec((tm, d), row),
            pl.BlockSpec((tm, p.shape[1]), row),
            pl.BlockSpec((1, d), fixed),
            pl.BlockSpec(wg.shape, fixed),
            pl.BlockSpec(wp.shape, fixed),
            pl.BlockSpec((1, d), fixed),
        ],
        out_specs=pl.BlockSpec((tm, d), row),
        out_shape=jax.ShapeDtypeStruct((n, d), F32),
        compiler_params=_cparams("arbitrary"),
    )(x2, p, gn, wg, wp, fn)


def _rot_cols(w):
    half = MLA_ROPE // 2
    return jnp.concatenate([-w[..., half:], w[..., :half]], axis=-1)


def _rope_table(pos):
    half = MLA_ROPE // 2
    inv = 1.0 / (ROPE_THETA ** (jnp.arange(half, dtype=F32) / half))
    ang = pos.astype(F32)[:, None] * inv[None, :]
    c, s = jnp.cos(ang), jnp.sin(ang)
    return jnp.concatenate([c, c, s, s], axis=-1)


def _layer_weights(l, norm_mix, w_in, hg_onorm, mla_q_norm, w_q_b, mla_kv_norm, w_kv_b, w_out,
                   norm_ffn, w_up, conv_w, conv_b, w_down, norm_ple, w_ple_gate, w_ple_proj):
    d = w_in.shape[1]
    wi = w_in[l]
    kr_cols = wi[:, HG_COLS + MLA_Q_LORA + MLA_KV_LORA:]
    pad = MLA_COLS - (MLA_Q_LORA + MLA_KV_LORA + 2 * MLA_ROPE)
    w_in_ext = jnp.concatenate([wi, _rot_cols(kr_cols), jnp.zeros((d, pad), wi.dtype)],
                               axis=1).astype(BF16)
    wq = w_q_b[l].reshape(MLA_Q_LORA, MLA_HEADS, MLA_NOPE + MLA_ROPE)
    wq_ext = jnp.concatenate([wq, _rot_cols(wq[..., MLA_NOPE:])], axis=-1)
    wq_ext = wq_ext.reshape(MLA_Q_LORA, MLA_HEADS * Q_HEAD_COLS).astype(BF16)
    wkv = w_kv_b[l].reshape(MLA_KV_LORA, MLA_HEADS, MLA_NOPE + MLA_V)
    wuk = jnp.transpose(wkv[..., :MLA_NOPE], (1, 2, 0)).astype(BF16)
    wuv = jnp.transpose(wkv[..., MLA_NOPE:], (1, 0, 2)).astype(BF16)
    wo = w_out[l].astype(BF16)
    hgw = HG_HEADS * HG_DV
    return dict(
        norm_mix=norm_mix[l][None], w_in=w_in_ext, hg_onorm=hg_onorm[l][None],
        q_norm=mla_q_norm[l][None], wq=wq_ext, kv_norm=mla_kv_norm[l][None], wuk=wuk, wuv=wuv,
        wo_hg=wo[:hgw], wo_mla=wo[hgw:], norm_ffn=norm_ffn[l][None], w_up=w_up[l].astype(BF16),
        conv_w=conv_w[l], conv_b=conv_b[l][None], w_down=w_down[l].astype(BF16),
        norm_ple=norm_ple[l][None], w_ple_gate=w_ple_gate[l].astype(BF16),
        w_ple_proj=w_ple_proj[l].astype(BF16))


def _trunk(x, p_l, w, hg_lower, norm_final, cs, *, layer, nb, t, final, hg_s0, attend, sample):
    proj = _in_proj(x, w["norm_mix"], w["w_in"])
    ckv, kr, kcat, q = _mla_proj(proj, cs, w["q_norm"], w["kv_norm"], w["wq"], w["wuk"],
                                 F32 if sample else BF16)
    o_hg, hg_s = _hgrn(proj, hg_lower, w["hg_onorm"], hg_s0, layer=layer, nb=nb, t=t,
                       seq_per_step=min(nb, 16) if sample else 1,
                       out_dtype=F32 if sample else BF16)
    o_mla = attend(q, ckv, kr, kcat, w["wuv"])
    x1, h2 = _out_proj(o_hg, o_mla, x, w["wo_hg"], w["wo_mla"], w["norm_ffn"])
    return x1, h2, ckv, kr, hg_s


def kernel(x_prompt, x_sample, cache_ckv, cache_krope, state_hgrn, state_conv, page_table,
           p_prompt, p_sample, norm_mix, w_in, hg_lower, hg_onorm, mla_q_norm, w_q_b,
           mla_kv_norm, w_kv_b, w_out, norm_ffn, w_up, conv_w, conv_b, w_down, norm_ple,
           w_ple_gate, w_ple_proj, norm_final):
    depth = w_in.shape[0]
    bp, tp, d = x_prompt.shape
    bs, ts, _ = x_sample.shape
    dff = w_down.shape[1]
    cs_p = _rope_table(jnp.arange(tp, dtype=jnp.int32))
    cs_s = jnp.tile(_rope_table(PAST_LEN + jnp.arange(ts, dtype=jnp.int32)), (bs, 1))
    nf = norm_final[None]
    xp = x_prompt.reshape(bp * tp, d)
    xs = x_sample.reshape(bs * ts, d)
    outs_p, outs_s = [], []
    for l in range(depth):
        w = _layer_weights(l, norm_mix, w_in, hg_onorm, mla_q_norm, w_q_b, mla_kv_norm, w_kv_b,
                           w_out, norm_ffn, w_up, conv_w, conv_b, w_down, norm_ple, w_ple_gate,
                           w_ple_proj)
        final = l == depth - 1

        attend_p = lambda q, ckv, kr, kcat, wuv: _prompt_attn(q, kcat, wuv, nb=bp, t=tp)
        x1, h2, ckv_p, kr_p, hg_p = _trunk(xp, None, w, hg_lower, nf, cs_p, layer=l, nb=bp, t=tp,
                                           final=final, hg_s0=None, attend=attend_p, sample=False)
        x2, tail = _ffn(h2, x1, w["w_up"], w["w_down"], w["conv_w"], w["conv_b"], None,
                        t=tp, t_sample=0)
        cv_p = tail.reshape(bp, -1, SUBLANES, dff)[:, -1, SUBLANES - (CONV_W - 1):]
        xp = _ple(x2, p_prompt[l].reshape(bp * tp, -1), w["norm_ple"], w["w_ple_gate"],
                  w["w_ple_proj"], nf, final=final)
        outs_p.append((ckv_p.reshape(bp, tp, -1), kr_p.reshape(bp, tp, -1), hg_p, cv_p))

        attend_s = lambda q, ckv, kr, kcat, wuv: _sample_attn(
            q, ckv, kr, wuv, cache_ckv, cache_krope, page_table, layer=l, nb=bs, t=ts)
        x1, h2, ckv_s, kr_s, hg_s = _trunk(xs, None, w, hg_lower, nf, cs_s, layer=l, nb=bs, t=ts,
                                           final=final, hg_s0=state_hgrn[l], attend=attend_s,
                                           sample=True)
        sp = jnp.pad(state_conv[l], ((0, 0), (ts - (CONV_W - 1), 0), (0, 0))).reshape(bs * ts, dff)
        x2, a_full = _ffn(h2, x1, w["w_up"], w["w_down"], w["conv_w"], w["conv_b"], sp,
                          t=ts, t_sample=ts)
        cv_s = a_full.reshape(bs, ts, dff)[:, ts - (CONV_W - 1):]
        xs = _ple(x2, p_sample[l].reshape(bs * ts, -1), w["norm_ple"], w["w_ple_gate"],
                  w["w_ple_proj"], nf, final=final)
        outs_s.append((ckv_s.reshape(bs, ts, -1), kr_s.reshape(bs, ts, -1), hg_s, cv_s))

    stack = lambda outs, k: jnp.stack([o[k] for o in outs])
    return (xp.reshape(bp, tp, d), xs.reshape(bs, ts, d),
            stack(outs_p, 0), stack(outs_p, 1), stack(outs_p, 2), stack(outs_p, 3),
            stack(outs_s, 0), stack(outs_s, 1), stack(outs_s, 2), stack(outs_s, 3))
```

```python
import functools

import jax
import jax.numpy as jnp
from jax import lax
from jax.experimental import pallas as pl
from jax.experimental.pallas import tpu as pltpu

F32 = jnp.float32
BF16 = jnp.bfloat16

HG_HEADS = 8
HG_DK = 128
HG_DV = 128
HG_CHUNK = 64
HG_GROUP_ROWS = 256
SAMPLE_PAGES_PER_GROUP = 32
ATTN_ROW_GROUPS = 2
MLA_HEADS = 8
MLA_Q_LORA = 512
MLA_KV_LORA = 256
MLA_NOPE = 128
MLA_ROPE = 64
MLA_V = 128
MLA_SCALE = (MLA_NOPE + MLA_ROPE) ** -0.5
ROPE_THETA = 10000.0
PAST_LEN = 16384
CONV_W = 3
EPS = 1e-6
NEG_INF = -1e30

HG_COLS = 4 * HG_HEADS * HG_DK
MLA_COLS = 1024
Q_HEAD_COLS = 256
QK_PAD = 384
LANES = 128
SUBLANES = 8

VMEM_LIMIT = 48 * 1024 * 1024

_NT = (((1,), (1,)), ((), ()))
_TN = (((0,), (0,)), ((), ()))


def _cparams(*sem):
    return pltpu.CompilerParams(dimension_semantics=sem, vmem_limit_bytes=VMEM_LIMIT)


def _tile(n, pref):
    t = min(n, pref)
    assert n % t == 0, (n, pref)
    return t


def _rms_rows(x, g):
    ms = jnp.mean(x * x, axis=-1, keepdims=True)
    return (x * lax.rsqrt(ms + EPS)) * g


def _dot(a, b):
    return jnp.dot(a, b, preferred_element_type=F32)


def _dot_nt(a, b):
    return lax.dot_general(a, b, _NT, preferred_element_type=F32)


def _in_proj_kernel(x_ref, g_ref, w_ref, o_ref, h_ref):
    @pl.when(pl.program_id(1) == 0)
    def _():
        h_ref[...] = _rms_rows(x_ref[...], g_ref[...]).astype(BF16)

    o_ref[...] = _dot(h_ref[...], w_ref[...])


def _in_proj(x, g, w):
    n, d = x.shape
    cols = w.shape[1]
    tm = _tile(n, 1024)
    tn = 1024
    return pl.pallas_call(
        _in_proj_kernel,
        name="in_proj",
        grid=(n // tm, cols // tn),
        in_specs=[
            pl.BlockSpec((tm, d), lambda i, j: (i, 0)),
            pl.BlockSpec((1, d), lambda i, j: (0, 0)),
            pl.BlockSpec((d, tn), lambda i, j: (0, j)),
        ],
        out_specs=pl.BlockSpec((tm, tn), lambda i, j: (i, j)),
        out_shape=jax.ShapeDtypeStruct((n, cols), F32),
        scratch_shapes=[pltpu.VMEM((tm, d), BF16)],
        compiler_params=_cparams("arbitrary", "arbitrary"),
    )(x, g, w)


def _mla_proj_kernel(p_ref, cs_ref, qn_ref, kvn_ref, wq_ref, wuk_ref,
                     ckv_ref, kr_ref, kcat_ref, q_ref):
    p = p_ref[...]
    cs = cs_ref[...]
    tm = p.shape[0]
    lane = lax.broadcasted_iota(jnp.int32, (tm, LANES), 1)
    cq = _rms_rows(p[:, :MLA_Q_LORA], qn_ref[...]).astype(BF16)
    ckv = _rms_rows(p[:, MLA_Q_LORA:MLA_Q_LORA + MLA_KV_LORA], kvn_ref[...])
    ckv_ref[...] = ckv
    kk = p[:, 768:896] * cs
    kro = kk + pltpu.roll(kk, MLA_ROPE, 1)
    kr_ref[...] = kro[:, :MLA_ROPE]
    kcat_ref[:, :MLA_KV_LORA] = ckv.astype(kcat_ref.dtype)
    kcat_ref[:, MLA_KV_LORA:] = jnp.where(lane < MLA_ROPE, kro, 0.0).astype(kcat_ref.dtype)
    q = _dot(cq, wq_ref[...])
    for h in range(MLA_HEADS):
        qh = q[:, h * Q_HEAD_COLS:(h + 1) * Q_HEAD_COLS]
        qlat = _dot(qh[:, :MLA_NOPE].astype(BF16), wuk_ref[h]) * MLA_SCALE
        rr = qh[:, MLA_NOPE:] * cs
        rope = (rr + pltpu.roll(rr, MLA_ROPE, 1)) * MLA_SCALE
        q_ref[:, h * QK_PAD:h * QK_PAD + MLA_KV_LORA] = qlat.astype(q_ref.dtype)
        q_ref[:, h * QK_PAD + MLA_KV_LORA:(h + 1) * QK_PAD] = (
            jnp.where(lane < MLA_ROPE, rope, 0.0).astype(q_ref.dtype))


def _mla_proj(proj, cs, qn, kvn, wq, wuk, q_dtype):
    n = proj.shape[0]
    tm = _tile(n, 256)
    cs_blocks = cs.shape[0] // tm
    mla_blk = HG_COLS // MLA_COLS
    return pl.pallas_call(
        _mla_proj_kernel,
        name="mla_proj",
        grid=(n // tm,),
        in_specs=[
            pl.BlockSpec((tm, MLA_COLS), lambda i: (i, mla_blk)),
            pl.BlockSpec((tm, LANES), lambda i: (i % cs_blocks, 0)),
            pl.BlockSpec((1, MLA_Q_LORA), lambda i: (0, 0)),
            pl.BlockSpec((1, MLA_KV_LORA), lambda i: (0, 0)),
            pl.BlockSpec(wq.shape, lambda i: (0, 0)),
            pl.BlockSpec(wuk.shape, lambda i: (0, 0, 0)),
        ],
        out_specs=[
            pl.BlockSpec((tm, MLA_KV_LORA), lambda i: (i, 0)),
            pl.BlockSpec((tm, MLA_ROPE), lambda i: (i, 0)),
            pl.BlockSpec((tm, QK_PAD), lambda i: (i, 0)),
            pl.BlockSpec((tm, MLA_HEADS * QK_PAD), lambda i: (i, 0)),
        ],
        out_shape=[
            jax.ShapeDtypeStruct((n, MLA_KV_LORA), F32),
            jax.ShapeDtypeStruct((n, MLA_ROPE), F32),
            jax.ShapeDtypeStruct((n, QK_PAD), BF16),
            jax.ShapeDtypeStruct((n, MLA_HEADS * QK_PAD), q_dtype),
        ],
        compiler_params=_cparams("arbitrary"),
    )(proj, cs, qn, kvn, wq, wuk)


def _hgrn_kernel(*refs, layer, nseq, nchunk, chunk, has_s0):
    if has_s0:
        hl_ref, q_ref, f_ref, i_ref, g_ref, on_ref, s0_ref, o_ref, sout_ref = refs
    else:
        hl_ref, q_ref, f_ref, i_ref, g_ref, on_ref, o_ref, sout_ref = refs
        s0_ref = None
    hl = hl_ref[...]
    e = jnp.exp(hl - jnp.max(hl, axis=0, keepdims=True))
    lb = jnp.sum(e[:layer + 1], axis=0, keepdims=True) / jnp.sum(e, axis=0, keepdims=True)
    onorm = on_ref[...]
    sup = min(HG_GROUP_ROWS // chunk, nseq * nchunk)
    rg = sup * chunk
    ngroups = (nseq * nchunk) // sup
    assert (nseq * nchunk) % sup == 0 and (nseq == 1 or nchunk == 1)
    rix = lax.broadcasted_iota(jnp.int32, (rg, rg), 0)
    cix = lax.broadcasted_iota(jnp.int32, (rg, rg), 1)
    same = (rix // chunk) == (cix // chunk)
    causal = jnp.logical_and(same, rix >= cix)
    scan_lhs = jnp.concatenate([causal, same], axis=0).astype(BF16)

    def group_body(s, st):
        r0 = pl.multiple_of(s * rg, rg)
        qr = q_ref[pl.ds(r0, rg), :]
        fx = f_ref[pl.ds(r0, rg), :]
        v = i_ref[pl.ds(r0, rg), :]
        gr = g_ref[pl.ds(r0, rg), :]
        sg = jax.nn.sigmoid(fx)
        f = lb + (1.0 - lb) * sg
        k = (1.0 - lb) * (1.0 - sg)
        logf = jnp.log(f)
        hi = logf.astype(BF16)
        r1 = logf - hi.astype(F32)
        mid = r1.astype(BF16)
        lo = (r1 - mid.astype(F32)).astype(BF16)
        parts = _dot(scan_lhs, jnp.concatenate([hi, mid, lo], axis=1))
        sums = (parts[:, :HG_DK] + parts[:, HG_DK:2 * HG_DK]) + parts[:, 2 * HG_DK:]
        cum = sums[:rg]
        tot = sums[rg:]
        qd = (qr * jax.nn.sigmoid(qr)) * jnp.exp(cum)
        ki = (k * jnp.exp(-cum)).astype(BF16)
        ke = k * jnp.exp(tot - cum)
        dec = jnp.exp(tot)
        a = jnp.where(causal, _dot_nt(qd.astype(BF16), ki), 0.0)
        o_intra = _dot(a.astype(BF16), v.astype(BF16))
        o_state = []
        for c in range(sup):
            cs = slice(c * chunk, (c + 1) * chunk)
            if nchunk == 1:
                g = s * sup + c
                st = s0_ref[g, 0].T if has_s0 else jnp.zeros((HG_DV, HG_DK), F32)
            qd_c = qd[cs].astype(BF16)
            o_state.append(_dot_nt(qd_c, st.astype(BF16)))
            upd = lax.dot_general(v[cs].astype(BF16), ke[cs].astype(BF16), _TN,
                                  preferred_element_type=F32)
            st = st * dec[c * chunk:c * chunk + 1] + upd
            if nchunk == 1:
                sout_ref[g, 0] = st.T
        o = o_intra + jnp.concatenate(o_state, axis=0)
        out = _rms_rows(o, onorm) * jax.nn.sigmoid(gr)
        o_ref[pl.ds(r0, rg), :] = out.astype(o_ref.dtype)
        return st

    if nchunk == 1:
        lax.fori_loop(0, ngroups, group_body, jnp.zeros((HG_DV, HG_DK), F32))
    else:
        st0 = s0_ref[0, 0].T if has_s0 else jnp.zeros((HG_DV, HG_DK), F32)
        sout_ref[0, 0] = lax.fori_loop(0, ngroups, group_body, st0).T


def _hgrn(proj, hg_lower, onorm, s0, *, layer, nb, t, seq_per_step, out_dtype):
    n = proj.shape[0]
    assert n == nb * t
    chunk = min(t, HG_CHUNK)
    assert t % chunk == 0 and nb % seq_per_step == 0
    rows = seq_per_step * t
    nl = hg_lower.shape[0]

    def col(seg):
        return pl.BlockSpec((rows, HG_DK), lambda i, h: (i, seg * HG_HEADS + h))

    in_specs = [
        pl.BlockSpec((nl, HG_DK), lambda i, h: (0, h)),
        col(0), col(1), col(2), col(3),
        pl.BlockSpec((1, HG_DV), lambda i, h: (0, h)),
    ]
    args = [hg_lower, proj, proj, proj, proj, onorm]
    state_spec = pl.BlockSpec((seq_per_step, 1, HG_DK, HG_DV), lambda i, h: (i, h, 0, 0))
    if s0 is not None:
        in_specs.append(state_spec)
        args.append(s0)
    kern = functools.partial(_hgrn_kernel, layer=layer, nseq=seq_per_step, nchunk=t // chunk,
                             chunk=chunk, has_s0=s0 is not None)
    return pl.pallas_call(
        kern,
        name="hgrn",
        grid=(nb // seq_per_step, HG_HEADS),
        in_specs=in_specs,
        out_specs=[
            pl.BlockSpec((rows, HG_DV), lambda i, h: (i, h)),
            state_spec,
        ],
        out_shape=[
            jax.ShapeDtypeStruct((n, HG_HEADS * HG_DV), out_dtype),
            jax.ShapeDtypeStruct((nb, HG_HEADS, HG_DK, HG_DV), F32),
        ],
        compiler_params=_cparams("arbitrary", "arbitrary"),
    )(*args)


def _prompt_attn_kernel(q_ref, k_ref, wuv_ref, o_ref, qs_ref, m_ref, l_ref, acc_ref, *, tq, tk):
    qi = pl.program_id(1)
    rows = MLA_HEADS * tq
    for h in range(MLA_HEADS):
        qs_ref[h * tq:(h + 1) * tq, :] = q_ref[:, h * QK_PAD:(h + 1) * QK_PAD]
    m_ref[...] = jnp.full((rows, 1), NEG_INF, F32)
    l_ref[...] = jnp.zeros((rows, 1), F32)
    acc_ref[...] = jnp.zeros((rows, MLA_KV_LORA), F32)
    nfull = (qi * tq) // tk
    rg = rows // ATTN_ROW_GROUPS

    def lane_fold(x, op):
        out = x[:, :LANES]
        for i in range(1, x.shape[1] // LANES):
            out = op(out, x[:, i * LANES:(i + 1) * LANES])
        return out

    def block(j, masked):
        kb = k_ref[pl.ds(pl.multiple_of(j * tk, tk), tk), :]
        for g in range(ATTN_ROW_GROUPS):
            sl = slice(g * rg, (g + 1) * rg)
            s = _dot_nt(qs_ref[sl, :], kb)
            if masked:
                tok = lax.rem(lax.broadcasted_iota(jnp.int32, (rg, tk), 0), tq)
                kidx = lax.broadcasted_iota(jnp.int32, (rg, tk), 1)
                s = jnp.where(kidx - tok <= qi * tq - j * tk, s, NEG_INF)
            m_prev = m_ref[sl, :]
            m_new = jnp.maximum(m_prev, jnp.max(lane_fold(s, jnp.maximum), axis=-1, keepdims=True))
            corr = jnp.exp(m_prev - m_new)
            p = jnp.exp(s - m_new)
            l_ref[sl, :] = l_ref[sl, :] * corr + jnp.sum(lane_fold(p, jnp.add), axis=-1,
                                                        keepdims=True)
            acc_ref[sl, :] = acc_ref[sl, :] * corr + _dot(p.astype(BF16), kb[:, :MLA_KV_LORA])
            m_ref[sl, :] = m_new

    def body(j, carry):
        block(j, False)
        return carry

    lax.fori_loop(0, nfull, body, 0)
    block(nfull, True)
    for h in range(MLA_HEADS):
        lat = acc_ref[h * tq:(h + 1) * tq, :] / l_ref[h * tq:(h + 1) * tq, :]
        o_ref[:, h * MLA_V:(h + 1) * MLA_V] = _dot(lat.astype(BF16), wuv_ref[h]).astype(o_ref.dtype)


def _prompt_attn(q, kcat, wuv, *, nb, t):
    n = q.shape[0]
    tq = _tile(t, 128)
    tk = _tile(t, 512)
    assert tk % tq == 0
    nq = t // tq
    rows = MLA_HEADS * tq
    return pl.pallas_call(
        functools.partial(_prompt_attn_kernel, tq=tq, tk=tk),
        name="prompt_attn",
        grid=(nb, nq),
        in_specs=[
            pl.BlockSpec((tq, MLA_HEADS * QK_PAD), lambda b, i: (b * nq + i, 0)),
            pl.BlockSpec((t, QK_PAD), lambda b, i: (b, 0)),
            pl.BlockSpec(wuv.shape, lambda b, i: (0, 0, 0)),
        ],
        out_specs=pl.BlockSpec((tq, MLA_HEADS * MLA_V), lambda b, i: (b * nq + i, 0)),
        out_shape=jax.ShapeDtypeStruct((n, MLA_HEADS * MLA_V), BF16),
        scratch_shapes=[
            pltpu.VMEM((rows, QK_PAD), BF16),
            pltpu.VMEM((rows, 1), F32),
            pltpu.VMEM((rows, 1), F32),
            pltpu.VMEM((rows, MLA_KV_LORA), F32),
        ],
        compiler_params=_cparams("arbitrary", "arbitrary"),
    )(q, kcat, wuv)


def _sample_attn_kernel(pt_ref, q_ref, ckvn_ref, krn_ref, wuv_ref, ckv_hbm, krt_hbm, o_ref,
                        kbuf, rbuf, sem, kc_ref, p_ref, qs_ref, knc_ref, knr_ref,
                        *, layer, n_pages, npg, nb, t):
    b = pl.program_id(0)
    ngroups = n_pages // npg
    rows = MLA_HEADS * t
    page = knc_ref.shape[0]

    def slot_of(g):
        return g % 2 if ngroups % 2 == 0 else lax.rem(b * ngroups + g, 2)

    def group_copies(bb, g, slot):
        copies = []
        for k in range(npg):
            pid = pt_ref[bb * n_pages + g * npg + k]
            copies.append(pltpu.make_async_copy(
                ckv_hbm.at[layer, pid], kbuf.at[slot, pl.ds(k * page, page), :], sem.at[slot]))
            copies.append(pltpu.make_async_copy(
                krt_hbm.at[layer, pid], rbuf.at[slot, :, pl.ds(k * page, page)], sem.at[slot]))
        return copies

    @pl.when(b == 0)
    def _():
        for c in group_copies(0, 0, slot_of(0)):
            c.start()

    for h in range(MLA_HEADS):
        qs_ref[h * t:(h + 1) * t, :] = q_ref[:, h * QK_PAD:(h + 1) * QK_PAD]
    qb = qs_ref[...].astype(BF16)
    ql = qb[:, :MLA_KV_LORA]
    qr = qb[:, MLA_KV_LORA:MLA_KV_LORA + MLA_ROPE]

    def softmax_step(s, m, l):
        m_new = jnp.maximum(m, jnp.max(s, axis=-1, keepdims=True))
        corr = jnp.exp(m - m_new)
        p = jnp.exp(s - m_new)
        return p.astype(BF16), corr, m_new, l * corr + jnp.sum(p, axis=-1, keepdims=True)

    m = jnp.full((rows, 1), NEG_INF, F32)
    l = jnp.zeros((rows, 1), F32)
    acc = jnp.zeros((rows, MLA_KV_LORA), F32)
    pending = None
    for g in range(ngroups):
        slot = slot_of(g)
        if g + 1 < ngroups:
            for c in group_copies(b, g + 1, 1 - slot):
                c.start()
        else:
            @pl.when(b + 1 < nb)
            def _():
                for c in group_copies(b + 1, 0, 1 - slot):
                    c.start()
        for c in group_copies(b, g, slot):
            c.wait()
        kc_ref[slot] = kbuf[slot].astype(BF16)
        s = _dot_nt(ql, kc_ref[slot]) + _dot(qr, rbuf[slot].astype(BF16))
        p, corr, m, l = softmax_step(s, m, l)
        p_ref[slot] = p
        if pending is not None:
            pslot, pcorr = pending
            acc = acc * pcorr + _dot(p_ref[pslot], kc_ref[pslot])
        pending = (slot, corr)
    pslot, pcorr = pending
    acc = acc * pcorr + _dot(p_ref[pslot], kc_ref[pslot])

    knc_ref[...] = jnp.zeros(knc_ref.shape, F32)
    knr_ref[...] = jnp.zeros(knr_ref.shape, F32)
    knc_ref[:t, :] = ckvn_ref[...]
    knr_ref[:t, :] = krn_ref[...]
    kc = knc_ref[...].astype(BF16)
    s = _dot_nt(ql, kc) + _dot_nt(qr, knr_ref[...].astype(BF16))
    tq = lax.rem(lax.broadcasted_iota(jnp.int32, (rows, page), 0), t)
    kidx = lax.broadcasted_iota(jnp.int32, (rows, page), 1)
    p, corr, m, l = softmax_step(jnp.where(kidx <= tq, s, NEG_INF), m, l)
    acc = acc * corr + _dot(p, kc)
    lat = acc / l
    for h in range(MLA_HEADS):
        o_ref[:, h * MLA_V:(h + 1) * MLA_V] = _dot(
            lat[h * t:(h + 1) * t].astype(BF16), wuv_ref[h]).astype(o_ref.dtype)


def _sample_attn(q, ckv_new, kr_new, wuv, cache_ckv, cache_krope, page_table, *, layer, nb, t):
    n_pages = page_table.shape[1]
    page = cache_ckv.shape[2]
    npg = min(n_pages, SAMPLE_PAGES_PER_GROUP)
    assert n_pages % npg == 0 and t <= page
    rows = MLA_HEADS * t
    pt_flat = page_table.reshape(-1)
    krope_t = jnp.swapaxes(cache_krope, 2, 3)
    in_specs = [
        pl.BlockSpec((t, MLA_HEADS * QK_PAD), lambda b, pt: (b, 0)),
        pl.BlockSpec((t, MLA_KV_LORA), lambda b, pt: (b, 0)),
        pl.BlockSpec((t, MLA_ROPE), lambda b, pt: (b, 0)),
        pl.BlockSpec(wuv.shape, lambda b, pt: (0, 0, 0)),
        pl.BlockSpec(memory_space=pl.ANY),
        pl.BlockSpec(memory_space=pl.ANY),
    ]
    grid_spec = pltpu.PrefetchScalarGridSpec(
        num_scalar_prefetch=1,
        grid=(nb,),
        in_specs=in_specs,
        out_specs=pl.BlockSpec((t, MLA_HEADS * MLA_V), lambda b, pt: (b, 0)),
        scratch_shapes=[
            pltpu.VMEM((2, npg * page, MLA_KV_LORA), F32),
            pltpu.VMEM((2, MLA_ROPE, npg * page), F32),
            pltpu.SemaphoreType.DMA((2,)),
            pltpu.VMEM((2, npg * page, MLA_KV_LORA), BF16),
            pltpu.VMEM((2, rows, npg * page), BF16),
            pltpu.VMEM((rows, QK_PAD), F32),
            pltpu.VMEM((page, MLA_KV_LORA), F32),
            pltpu.VMEM((page, MLA_ROPE), F32),
        ],
    )
    return pl.pallas_call(
        functools.partial(_sample_attn_kernel, layer=layer, n_pages=n_pages, npg=npg, nb=nb, t=t),
        name="sample_attn",
        grid_spec=grid_spec,
        out_shape=jax.ShapeDtypeStruct((nb * t, MLA_HEADS * MLA_V), F32),
        compiler_params=_cparams("arbitrary"),
    )(pt_flat, q, ckv_new, kr_new, wuv, cache_ckv, krope_t)


def _out_proj_kernel(ohg_ref, omla_ref, x_ref, w1_ref, w2_ref, g_ref, x1_ref, h2_ref):
    y = _dot(ohg_ref[...].astype(BF16), w1_ref[...]) + _dot(omla_ref[...].astype(BF16), w2_ref[...])
    x1 = x_ref[...] + y
    x1_ref[...] = x1
    h2_ref[...] = _rms_rows(x1, g_ref[...]).astype(BF16)


def _out_proj(ohg, omla, x, w1, w2, g):
    n, d = x.shape
    tm = _tile(n, 512)
    row = lambda i: (i, 0)
    fixed = lambda i: (0, 0)
    return pl.pallas_call(
        _out_proj_kernel,
        name="out_proj",
        grid=(n // tm,),
        in_specs=[
            pl.BlockSpec((tm, ohg.shape[1]), row),
            pl.BlockSpec((tm, omla.shape[1]), row),
            pl.BlockSpec((tm, d), row),
            pl.BlockSpec(w1.shape, fixed),
            pl.BlockSpec(w2.shape, fixed),
            pl.BlockSpec((1, d), fixed),
        ],
        out_specs=[pl.BlockSpec((tm, d), row), pl.BlockSpec((tm, d), row)],
        out_shape=[jax.ShapeDtypeStruct((n, d), F32), jax.ShapeDtypeStruct((n, d), BF16)],
        compiler_params=_cparams("arbitrary"),
    )(ohg, omla, x, w1, w2, g)


def _ffn_kernel(*refs, tm, tiles_per_seq, t_sample):
    if t_sample:
        h_ref, x_ref, wa_ref, wg_ref, wd_ref, cw_ref, cb_ref, sp_ref, o_ref, tail_ref = refs
    else:
        h_ref, x_ref, wa_ref, wg_ref, wd_ref, cw_ref, cb_ref, o_ref, tail_ref, halo_ref = refs
    i = pl.program_id(0)
    c = pl.program_id(1)

    @pl.when(c == 0)
    def _():
        o_ref[...] = x_ref[...]

    h = h_ref[...]
    a = _dot(h, wa_ref[...])
    gate = _dot(h, wg_ref[...])
    tc = a.shape[1]
    r1 = pltpu.roll(a, 1, 0)
    r2 = pltpu.roll(a, 2, 0)
    if t_sample:
        sp = sp_ref[...]
        tt = lax.rem(lax.broadcasted_iota(jnp.int32, (tm, tc), 0), t_sample)
        a1 = jnp.where(tt == 0, pltpu.roll(sp, tm - (t_sample - 1), 0), r1)
        a2 = jnp.where(tt < 2, pltpu.roll(sp, tm - (t_sample - 2), 0), r2)
        tail_ref[...] = a
    else:
        @pl.when(lax.rem(i, tiles_per_seq) == 0)
        def _():
            halo_ref[c] = jnp.zeros((SUBLANES, tc), F32)

        halo = halo_ref[c]
        row8 = lax.broadcasted_iota(jnp.int32, (SUBLANES, tc), 0)
        top1 = jnp.where(row8 < 1, pltpu.roll(halo, 1, 0), r1[:SUBLANES])
        top2 = jnp.where(row8 < 2, pltpu.roll(halo, 2, 0), r2[:SUBLANES])
        a1 = jnp.concatenate([top1, r1[SUBLANES:]], axis=0)
        a2 = jnp.concatenate([top2, r2[SUBLANES:]], axis=0)
        halo_ref[c] = a[tm - SUBLANES:]
        tail_ref[0] = a[tm - SUBLANES:]
    cw = cw_ref[...]
    conv = ((cb_ref[...] + cw[0:1] * a2) + cw[1:2] * a1) + cw[2:3] * a
    act = (conv * jax.nn.sigmoid(conv)) * gate
    o_ref[...] += _dot(act.astype(BF16), wd_ref[...])


def _ffn(h2, x1, w_up, w_down, conv_w, conv_b, sp, *, t, t_sample):
    n, d = x1.shape
    dff = w_down.shape[0]
    tc = 512
    nc = dff // tc
    assert dff % tc == 0
    if t_sample:
        tm = _tile(n, 512)
        assert tm % t_sample == 0
        tiles_per_seq = 1
        tail_shape = (n, dff)
        tail_spec = pl.BlockSpec((tm, tc), lambda i, c: (i, c))
    else:
        tm = _tile(t, 512)
        tiles_per_seq = t // tm
        tail_shape = (n // tm, SUBLANES, dff)
        tail_spec = pl.BlockSpec((1, SUBLANES, tc), lambda i, c: (i, 0, c))
    in_specs = [
        pl.BlockSpec((tm, d), lambda i, c: (i, 0)),
        pl.BlockSpec((tm, d), lambda i, c: (i, 0)),
        pl.BlockSpec((d, tc), lambda i, c: (0, c)),
        pl.BlockSpec((d, tc), lambda i, c: (0, nc + c)),
        pl.BlockSpec((tc, d), lambda i, c: (c, 0)),
        pl.BlockSpec((CONV_W, tc), lambda i, c: (0, c)),
        pl.BlockSpec((1, tc), lambda i, c: (0, c)),
    ]
    args = [h2, x1, w_up, w_up, w_down, conv_w, conv_b]
    scratch = []
    if t_sample:
        in_specs.append(pl.BlockSpec((tm, tc), lambda i, c: (i, c)))
        args.append(sp)
    else:
        scratch.append(pltpu.VMEM((nc, SUBLANES, tc), F32))
    return pl.pallas_call(
        functools.partial(_ffn_kernel, tm=tm, tiles_per_seq=tiles_per_seq, t_sample=t_sample),
        name="ffn",
        grid=(n // tm, nc),
        in_specs=in_specs,
        out_specs=[pl.BlockSpec((tm, d), lambda i, c: (i, 0)), tail_spec],
        out_shape=[jax.ShapeDtypeStruct((n, d), F32), jax.ShapeDtypeStruct(tail_shape, F32)],
        scratch_shapes=scratch,
        compiler_params=_cparams("arbitrary", "arbitrary"),
    )(*args)


def _ple_kernel(x_ref, p_ref, gn_ref, wg_ref, wp_ref, fn_ref, y_ref, *, final):
    x = x_ref[...]
    hn = _rms_rows(x, gn_ref[...]).astype(BF16)
    gate = jax.nn.sigmoid(_dot(hn, wg_ref[...]))
    x3 = x + gate * _dot(p_ref[...].astype(BF16), wp_ref[...])
    y_ref[...] = _rms_rows(x3, fn_ref[...]) if final else x3


def _ple(x2, p, gn, wg, wp, fn, *, final):
    n, d = x2.shape
    tm = _tile(n, 512)
    row = lambda i: (i, 0)
    fixed = lambda i: (0, 0)
    return pl.pallas_call(
        functools.partial(_ple_kernel, final=final),
        name="ple",
        grid=(n // tm,),
        in_specs=[
            pl.BlockSpec((tm, d), row),
            pl.BlockSpec((tm, p.shape[1]), row),
            pl.BlockSpec((1, d), fixed),
            pl.BlockSpec(wg.shape, fixed),
            pl.BlockSpec(wp.shape, fixed),
            pl.BlockSpec((1, d), fixed),
        ],
        out_specs=pl.BlockSpec((tm, d), row),
        out_shape=jax.ShapeDtypeStruct((n, d), F32),
        compiler_params=_cparams("arbitrary"),
    )(x2, p, gn, wg, wp, fn)


def _rot_cols(w):
    half = MLA_ROPE // 2
    return jnp.concatenate([-w[..., half:], w[..., :half]], axis=-1)


def _rope_table(pos):
    half = MLA_ROPE // 2
    inv = 1.0 / (ROPE_THETA ** (jnp.arange(half, dtype=F32) / half))
    ang = pos.astype(F32)[:, None] * inv[None, :]
    c, s = jnp.cos(ang), jnp.sin(ang)
    return jnp.concatenate([c, c, s, s], axis=-1)


def _layer_weights(l, norm_mix, w_in, hg_onorm, mla_q_norm, w_q_b, mla_kv_norm, w_kv_b, w_out,
                   norm_ffn, w_up, conv_w, conv_b, w_down, norm_ple, w_ple_gate, w_ple_proj):
    d = w_in.shape[1]
    wi = w_in[l]
    kr_cols = wi[:, HG_COLS + MLA_Q_LORA + MLA_KV_LORA:]
    pad = MLA_COLS - (MLA_Q_LORA + MLA_KV_LORA + 2 * MLA_ROPE)
    w_in_ext = jnp.concatenate([wi, _rot_cols(kr_cols), jnp.zeros((d, pad), wi.dtype)],
                               axis=1).astype(BF16)
    wq = w_q_b[l].reshape(MLA_Q_LORA, MLA_HEADS, MLA_NOPE + MLA_ROPE)
    wq_ext = jnp.concatenate([wq, _rot_cols(wq[..., MLA_NOPE:])], axis=-1)
    wq_ext = wq_ext.reshape(MLA_Q_LORA, MLA_HEADS * Q_HEAD_COLS).astype(BF16)
    wkv = w_kv_b[l].reshape(MLA_KV_LORA, MLA_HEADS, MLA_NOPE + MLA_V)
    wuk = jnp.transpose(wkv[..., :MLA_NOPE], (1, 2, 0)).astype(BF16)
    wuv = jnp.transpose(wkv[..., MLA_NOPE:], (1, 0, 2)).astype(BF16)
    wo = w_out[l].astype(BF16)
    hgw = HG_HEADS * HG_DV
    return dict(
        norm_mix=norm_mix[l][None], w_in=w_in_ext, hg_onorm=hg_onorm[l][None],
        q_norm=mla_q_norm[l][None], wq=wq_ext, kv_norm=mla_kv_norm[l][None], wuk=wuk, wuv=wuv,
        wo_hg=wo[:hgw], wo_mla=wo[hgw:], norm_ffn=norm_ffn[l][None], w_up=w_up[l].astype(BF16),
        conv_w=conv_w[l], conv_b=conv_b[l][None], w_down=w_down[l].astype(BF16),
        norm_ple=norm_ple[l][None], w_ple_gate=w_ple_gate[l].astype(BF16),
        w_ple_proj=w_ple_proj[l].astype(BF16))


def _trunk(x, p_l, w, hg_lower, norm_final, cs, *, layer, nb, t, final, hg_s0, attend, sample):
    proj = _in_proj(x, w["norm_mix"], w["w_in"])
    ckv, kr, kcat, q = _mla_proj(proj, cs, w["q_norm"], w["kv_norm"], w["wq"], w["wuk"],
                                 F32 if sample else BF16)
    o_hg, hg_s = _hgrn(proj, hg_lower, w["hg_onorm"], hg_s0, layer=layer, nb=nb, t=t,
                       seq_per_step=min(nb, 16) if sample else 1,
                       out_dtype=F32 if sample else BF16)
    o_mla = attend(q, ckv, kr, kcat, w["wuv"])
    x1, h2 = _out_proj(o_hg, o_mla, x, w["wo_hg"], w["wo_mla"], w["norm_ffn"])
    return x1, h2, ckv, kr, hg_s


def kernel(x_prompt, x_sample, cache_ckv, cache_krope, state_hgrn, state_conv, page_table,
           p_prompt, p_sample, norm_mix, w_in, hg_lower, hg_onorm, mla_q_norm, w_q_b,
           mla_kv_norm, w_kv_b, w_out, norm_ffn, w_up, conv_w, conv_b, w_down, norm_ple,
           w_ple_gate, w_ple_proj, norm_final):
    depth = w_in.shape[0]
    bp, tp, d = x_prompt.shape
    bs, ts, _ = x_sample.shape
    dff = w_down.shape[1]
    cs_p = _rope_table(jnp.arange(tp, dtype=jnp.int32))
    cs_s = jnp.tile(_rope_table(PAST_LEN + jnp.arange(ts, dtype=jnp.int32)), (bs, 1))
    nf = norm_final[None]
    xp = x_prompt.reshape(bp * tp, d)
    xs = x_sample.reshape(bs * ts, d)
    outs_p, outs_s = [], []
    for l in range(depth):
        w = _layer_weights(l, norm_mix, w_in, hg_onorm, mla_q_norm, w_q_b, mla_kv_norm, w_kv_b,
                           w_out, norm_ffn, w_up, conv_w, conv_b, w_down, norm_ple, w_ple_gate,
                           w_ple_proj)
        final = l == depth - 1

        attend_p = lambda q, ckv, kr, kcat, wuv: _prompt_attn(q, kcat, wuv, nb=bp, t=tp)
        x1, h2, ckv_p, kr_p, hg_p = _trunk(xp, None, w, hg_lower, nf, cs_p, layer=l, nb=bp, t=tp,
                                           final=final, hg_s0=None, attend=attend_p, sample=False)
        x2, tail = _ffn(h2, x1, w["w_up"], w["w_down"], w["conv_w"], w["conv_b"], None,
                        t=tp, t_sample=0)
        cv_p = tail.reshape(bp, -1, SUBLANES, dff)[:, -1, SUBLANES - (CONV_W - 1):]
        xp = _ple(x2, p_prompt[l].reshape(bp * tp, -1), w["norm_ple"], w["w_ple_gate"],
                  w["w_ple_proj"], nf, final=final)
        outs_p.append((ckv_p.reshape(bp, tp, -1), kr_p.reshape(bp, tp, -1), hg_p, cv_p))

        attend_s = lambda q, ckv, kr, kcat, wuv: _sample_attn(
            q, ckv, kr, wuv, cache_ckv, cache_krope, page_table, layer=l, nb=bs, t=ts)
        x1, h2, ckv_s, kr_s, hg_s = _trunk(xs, None, w, hg_lower, nf, cs_s, layer=l, nb=bs, t=ts,
                                           final=final, hg_s0=state_hgrn[l], attend=attend_s,
                                           sample=True)
        sp = jnp.pad(state_conv[l], ((0, 0), (ts - (CONV_W - 1), 0), (0, 0))).reshape(bs * ts, dff)
        x2, a_full = _ffn(h2, x1, w["w_up"], w["w_down"], w["conv_w"], w["conv_b"], sp,
                          t=ts, t_sample=ts)
        cv_s = a_full.reshape(bs, ts, dff)[:, ts - (CONV_W - 1):]
        xs = _ple(x2, p_sample[l].reshape(bs * ts, -1), w["norm_ple"], w["w_ple_gate"],
                  w["w_ple_proj"], nf, final=final)
        outs_s.append((ckv_s.reshape(bs, ts, -1), kr_s.reshape(bs, ts, -1), hg_s, cv_s))

    stack = lambda outs, k: jnp.stack([o[k] for o in outs])
    return (xp.reshape(bp, tp, d), xs.reshape(bs, ts, d),
            stack(outs_p, 0), stack(outs_p, 1), stack(outs_p, 2), stack(outs_p, 3),
            stack(outs_s, 0), stack(outs_s, 1), stack(outs_s, 2), stack(outs_s, 3))
```

```python
import functools

import jax
import jax.numpy as jnp
from jax import lax
from jax.experimental import pallas as pl
from jax.experimental.pallas import tpu as pltpu

F32 = jnp.float32
BF16 = jnp.bfloat16

HG_HEADS = 8
HG_DK = 128
HG_DV = 128
HG_CHUNK = 64
HG_GROUP_ROWS = 256
SAMPLE_PAGES_PER_GROUP = 32
ATTN_SOFTMAX_ROWS = 32
ATTN_SOFTMAX_UNROLL = 32
MLA_HEADS = 8
MLA_Q_LORA = 512
MLA_KV_LORA = 256
MLA_NOPE = 128
MLA_ROPE = 64
MLA_V = 128
MLA_SCALE = (MLA_NOPE + MLA_ROPE) ** -0.5
ROPE_THETA = 10000.0
PAST_LEN = 16384
CONV_W = 3
EPS = 1e-6
NEG_INF = -1e30

HG_COLS = 4 * HG_HEADS * HG_DK
MLA_COLS = 1024
Q_HEAD_COLS = 256
QK_PAD = 384
LANES = 128
SUBLANES = 8

VMEM_LIMIT = 48 * 1024 * 1024

_NT = (((1,), (1,)), ((), ()))
_TN = (((0,), (0,)), ((), ()))


def _cparams(*sem):
    return pltpu.CompilerParams(dimension_semantics=sem, vmem_limit_bytes=VMEM_LIMIT)


def _tile(n, pref):
    t = min(n, pref)
    assert n % t == 0, (n, pref)
    return t


def _rms_rows(x, g):
    ms = jnp.mean(x * x, axis=-1, keepdims=True)
    return (x * lax.rsqrt(ms + EPS)) * g


def _dot(a, b):
    return jnp.dot(a, b, preferred_element_type=F32)


def _dot_nt(a, b):
    return lax.dot_general(a, b, _NT, preferred_element_type=F32)


def _in_proj_kernel(x_ref, g_ref, w_ref, o_ref, h_ref):
    @pl.when(pl.program_id(1) == 0)
    def _():
        h_ref[...] = _rms_rows(x_ref[...], g_ref[...]).astype(BF16)

    o_ref[...] = _dot(h_ref[...], w_ref[...])


def _in_proj(x, g, w):
    n, d = x.shape
    cols = w.shape[1]
    tm = _tile(n, 1024)
    tn = 1024
    return pl.pallas_call(
        _in_proj_kernel,
        name="in_proj",
        grid=(n // tm, cols // tn),
        in_specs=[
            pl.BlockSpec((tm, d), lambda i, j: (i, 0)),
            pl.BlockSpec((1, d), lambda i, j: (0, 0)),
            pl.BlockSpec((d, tn), lambda i, j: (0, j)),
        ],
        out_specs=pl.BlockSpec((tm, tn), lambda i, j: (i, j)),
        out_shape=jax.ShapeDtypeStruct((n, cols), F32),
        scratch_shapes=[pltpu.VMEM((tm, d), BF16)],
        compiler_params=_cparams("arbitrary", "arbitrary"),
    )(x, g, w)


def _mla_proj_kernel(p_ref, cs_ref, qn_ref, kvn_ref, wq_ref, wuk_ref,
                     ckv_ref, kr_ref, kcat_ref, q_ref):
    p = p_ref[...]
    cs = cs_ref[...]
    tm = p.shape[0]
    lane = lax.broadcasted_iota(jnp.int32, (tm, LANES), 1)
    cq = _rms_rows(p[:, :MLA_Q_LORA], qn_ref[...]).astype(BF16)
    ckv = _rms_rows(p[:, MLA_Q_LORA:MLA_Q_LORA + MLA_KV_LORA], kvn_ref[...])
    ckv_ref[...] = ckv
    kk = p[:, 768:896] * cs
    kro = kk + pltpu.roll(kk, MLA_ROPE, 1)
    kr_ref[...] = kro[:, :MLA_ROPE]
    kcat_ref[:, :MLA_KV_LORA] = ckv.astype(kcat_ref.dtype)
    kcat_ref[:, MLA_KV_LORA:] = jnp.where(lane < MLA_ROPE, kro, 0.0).astype(kcat_ref.dtype)
    q = _dot(cq, wq_ref[...])
    for h in range(MLA_HEADS):
        qh = q[:, h * Q_HEAD_COLS:(h + 1) * Q_HEAD_COLS]
        qlat = _dot(qh[:, :MLA_NOPE].astype(BF16), wuk_ref[h]) * MLA_SCALE
        rr = qh[:, MLA_NOPE:] * cs
        rope = (rr + pltpu.roll(rr, MLA_ROPE, 1)) * MLA_SCALE
        q_ref[:, h * QK_PAD:h * QK_PAD + MLA_KV_LORA] = qlat.astype(q_ref.dtype)
        q_ref[:, h * QK_PAD + MLA_KV_LORA:(h + 1) * QK_PAD] = (
            jnp.where(lane < MLA_ROPE, rope, 0.0).astype(q_ref.dtype))


def _mla_proj(proj, cs, qn, kvn, wq, wuk, q_dtype):
    n = proj.shape[0]
    tm = _tile(n, 256)
    cs_blocks = cs.shape[0] // tm
    mla_blk = HG_COLS // MLA_COLS
    return pl.pallas_call(
        _mla_proj_kernel,
        name="mla_proj",
        grid=(n // tm,),
        in_specs=[
            pl.BlockSpec((tm, MLA_COLS), lambda i: (i, mla_blk)),
            pl.BlockSpec((tm, LANES), lambda i: (i % cs_blocks, 0)),
            pl.BlockSpec((1, MLA_Q_LORA), lambda i: (0, 0)),
            pl.BlockSpec((1, MLA_KV_LORA), lambda i: (0, 0)),
            pl.BlockSpec(wq.shape, lambda i: (0, 0)),
            pl.BlockSpec(wuk.shape, lambda i: (0, 0, 0)),
        ],
        out_specs=[
            pl.BlockSpec((tm, MLA_KV_LORA), lambda i: (i, 0)),
            pl.BlockSpec((tm, MLA_ROPE), lambda i: (i, 0)),
            pl.BlockSpec((tm, QK_PAD), lambda i: (i, 0)),
            pl.BlockSpec((tm, MLA_HEADS * QK_PAD), lambda i: (i, 0)),
        ],
        out_shape=[
            jax.ShapeDtypeStruct((n, MLA_KV_LORA), F32),
            jax.ShapeDtypeStruct((n, MLA_ROPE), F32),
            jax.ShapeDtypeStruct((n, QK_PAD), BF16),
            jax.ShapeDtypeStruct((n, MLA_HEADS * QK_PAD), q_dtype),
        ],
        compiler_params=_cparams("arbitrary"),
    )(proj, cs, qn, kvn, wq, wuk)


def _hgrn_kernel(*refs, layer, nseq, nchunk, chunk, has_s0):
    if has_s0:
        hl_ref, q_ref, f_ref, i_ref, g_ref, on_ref, s0_ref, o_ref, sout_ref = refs
    else:
        hl_ref, q_ref, f_ref, i_ref, g_ref, on_ref, o_ref, sout_ref = refs
        s0_ref = None
    hl = hl_ref[...]
    e = jnp.exp(hl - jnp.max(hl, axis=0, keepdims=True))
    lb = jnp.sum(e[:layer + 1], axis=0, keepdims=True) / jnp.sum(e, axis=0, keepdims=True)
    onorm = on_ref[...]
    sup = min(HG_GROUP_ROWS // chunk, nseq * nchunk)
    rg = sup * chunk
    ngroups = (nseq * nchunk) // sup
    assert (nseq * nchunk) % sup == 0 and (nseq == 1 or nchunk == 1)
    rix = lax.broadcasted_iota(jnp.int32, (rg, rg), 0)
    cix = lax.broadcasted_iota(jnp.int32, (rg, rg), 1)
    same = (rix // chunk) == (cix // chunk)
    causal = jnp.logical_and(same, rix >= cix)
    scan_lhs = jnp.concatenate([causal, same], axis=0).astype(BF16)

    def group_body(s, st):
        r0 = pl.multiple_of(s * rg, rg)
        qr = q_ref[pl.ds(r0, rg), :]
        fx = f_ref[pl.ds(r0, rg), :]
        v = i_ref[pl.ds(r0, rg), :]
        gr = g_ref[pl.ds(r0, rg), :]
        sg = jax.nn.sigmoid(fx)
        f = lb + (1.0 - lb) * sg
        k = (1.0 - lb) * (1.0 - sg)
        logf = jnp.log(f)
        hi = logf.astype(BF16)
        r1 = logf - hi.astype(F32)
        mid = r1.astype(BF16)
        lo = (r1 - mid.astype(F32)).astype(BF16)
        parts = _dot(scan_lhs, jnp.concatenate([hi, mid, lo], axis=1))
        sums = (parts[:, :HG_DK] + parts[:, HG_DK:2 * HG_DK]) + parts[:, 2 * HG_DK:]
        cum = sums[:rg]
        tot = sums[rg:]
        qd = (qr * jax.nn.sigmoid(qr)) * jnp.exp(cum)
        ki = (k * jnp.exp(-cum)).astype(BF16)
        ke = k * jnp.exp(tot - cum)
        dec = jnp.exp(tot)
        a = jnp.where(causal, _dot_nt(qd.astype(BF16), ki), 0.0)
        o_intra = _dot(a.astype(BF16), v.astype(BF16))
        o_state = []
        for c in range(sup):
            cs = slice(c * chunk, (c + 1) * chunk)
            if nchunk == 1:
                g = s * sup + c
                st = s0_ref[g, 0].T if has_s0 else jnp.zeros((HG_DV, HG_DK), F32)
            qd_c = qd[cs].astype(BF16)
            o_state.append(_dot_nt(qd_c, st.astype(BF16)))
            upd = lax.dot_general(v[cs].astype(BF16), ke[cs].astype(BF16), _TN,
                                  preferred_element_type=F32)
            st = st * dec[c * chunk:c * chunk + 1] + upd
            if nchunk == 1:
                sout_ref[g, 0] = st.T
        o = o_intra + jnp.concatenate(o_state, axis=0)
        out = _rms_rows(o, onorm) * jax.nn.sigmoid(gr)
        o_ref[pl.ds(r0, rg), :] = out.astype(o_ref.dtype)
        return st

    if nchunk == 1:
        lax.fori_loop(0, ngroups, group_body, jnp.zeros((HG_DV, HG_DK), F32))
    else:
        st0 = s0_ref[0, 0].T if has_s0 else jnp.zeros((HG_DV, HG_DK), F32)
        sout_ref[0, 0] = lax.fori_loop(0, ngroups, group_body, st0).T


def _hgrn(proj, hg_lower, onorm, s0, *, layer, nb, t, seq_per_step, out_dtype):
    n = proj.shape[0]
    assert n == nb * t
    chunk = min(t, HG_CHUNK)
    assert t % chunk == 0 and nb % seq_per_step == 0
    rows = seq_per_step * t
    nl = hg_lower.shape[0]

    def col(seg):
        return pl.BlockSpec((rows, HG_DK), lambda i, h: (i, seg * HG_HEADS + h))

    in_specs = [
        pl.BlockSpec((nl, HG_DK), lambda i, h: (0, h)),
        col(0), col(1), col(2), col(3),
        pl.BlockSpec((1, HG_DV), lambda i, h: (0, h)),
    ]
    args = [hg_lower, proj, proj, proj, proj, onorm]
    state_spec = pl.BlockSpec((seq_per_step, 1, HG_DK, HG_DV), lambda i, h: (i, h, 0, 0))
    if s0 is not None:
        in_specs.append(state_spec)
        args.append(s0)
    kern = functools.partial(_hgrn_kernel, layer=layer, nseq=seq_per_step, nchunk=t // chunk,
                             chunk=chunk, has_s0=s0 is not None)
    return pl.pallas_call(
        kern,
        name="hgrn",
        grid=(nb // seq_per_step, HG_HEADS),
        in_specs=in_specs,
        out_specs=[
            pl.BlockSpec((rows, HG_DV), lambda i, h: (i, h)),
            state_spec,
        ],
        out_shape=[
            jax.ShapeDtypeStruct((n, HG_HEADS * HG_DV), out_dtype),
            jax.ShapeDtypeStruct((nb, HG_HEADS, HG_DK, HG_DV), F32),
        ],
        compiler_params=_cparams("arbitrary", "arbitrary"),
    )(*args)


def _prompt_attn_kernel(q_ref, k_ref, wuv_ref, o_ref, qs_ref, s_ref, p_ref, m_ref, l_ref, acc_ref,
                        *, tq, tk):
    qi = pl.program_id(1)
    rows = MLA_HEADS * tq
    nlg = tk // LANES
    rb = ATTN_SOFTMAX_ROWS
    for h in range(MLA_HEADS):
        qs_ref[h * tq:(h + 1) * tq, :] = q_ref[:, h * QK_PAD:(h + 1) * QK_PAD]
    m_ref[...] = jnp.full((rows, LANES), NEG_INF, F32)
    l_ref[...] = jnp.zeros((rows, LANES), F32)
    acc_ref[...] = jnp.zeros((rows, MLA_KV_LORA), F32)
    nfull = (qi * tq) // tk

    def softmax_rows(r, limit):
        rs = pl.ds(pl.multiple_of(r * rb, rb), rb)
        sb = [s_ref[rs, i * LANES:(i + 1) * LANES] for i in range(nlg)]
        if limit is not None:
            tok = lax.rem(r * rb + lax.broadcasted_iota(jnp.int32, (rb, LANES), 0), tq)
            lane = lax.broadcasted_iota(jnp.int32, (rb, LANES), 1)
            sb = [jnp.where(lane + i * LANES - tok <= limit, sb[i], NEG_INF) for i in range(nlg)]
        mx = functools.reduce(jnp.maximum, sb)
        m_prev = m_ref[rs, :]
        m_new = jnp.maximum(m_prev, jnp.max(mx, axis=-1, keepdims=True))
        corr = jnp.exp(m_prev - m_new)
        ps = [jnp.exp(x - m_new) for x in sb]
        l_ref[rs, :] = l_ref[rs, :] * corr + jnp.sum(functools.reduce(jnp.add, ps), axis=-1,
                                                     keepdims=True)
        m_ref[rs, :] = m_new
        for i in range(nlg):
            p_ref[rs, i * LANES:(i + 1) * LANES] = ps[i].astype(BF16)
        for i in range(MLA_KV_LORA // LANES):
            acc_ref[rs, i * LANES:(i + 1) * LANES] = acc_ref[rs, i * LANES:(i + 1) * LANES] * corr

    def block(j, masked):
        kb = k_ref[pl.ds(pl.multiple_of(j * tk, tk), tk), :]
        hk = tk // 2
        s_ref[:, :hk] = _dot_nt(qs_ref[...], kb[:hk])
        s_ref[:, hk:] = _dot_nt(qs_ref[...], kb[hk:])
        limit = qi * tq - j * tk if masked else None

        def sweep(r, carry):
            softmax_rows(r, limit)
            return carry

        lax.fori_loop(0, rows // rb, sweep, 0, unroll=ATTN_SOFTMAX_UNROLL)
        hr = rows // 2
        acc_ref[:hr, :] += _dot(p_ref[:hr, :], kb[:, :MLA_KV_LORA])
        acc_ref[hr:, :] += _dot(p_ref[hr:, :], kb[:, :MLA_KV_LORA])

    def body(j, carry):
        block(j, False)
        return carry

    lax.fori_loop(0, nfull, body, 0)
    block(nfull, True)
    for h in range(MLA_HEADS):
        hs = slice(h * tq, (h + 1) * tq)
        inv = 1.0 / l_ref[hs, :]
        lat = jnp.concatenate([acc_ref[hs, i * LANES:(i + 1) * LANES] * inv
                               for i in range(MLA_KV_LORA // LANES)], axis=1)
        o_ref[:, h * MLA_V:(h + 1) * MLA_V] = _dot(lat.astype(BF16), wuv_ref[h]).astype(o_ref.dtype)


def _prompt_attn(q, kcat, wuv, *, nb, t):
    n = q.shape[0]
    tq = _tile(t, 128)
    tk = _tile(t, 512)
    assert tk % tq == 0
    nq = t // tq
    rows = MLA_HEADS * tq
    return pl.pallas_call(
        functools.partial(_prompt_attn_kernel, tq=tq, tk=tk),
        name="prompt_attn",
        grid=(nb, nq),
        in_specs=[
            pl.BlockSpec((tq, MLA_HEADS * QK_PAD), lambda b, i: (b * nq + i, 0)),
            pl.BlockSpec((t, QK_PAD), lambda b, i: (b, 0)),
            pl.BlockSpec(wuv.shape, lambda b, i: (0, 0, 0)),
        ],
        out_specs=pl.BlockSpec((tq, MLA_HEADS * MLA_V), lambda b, i: (b * nq + i, 0)),
        out_shape=jax.ShapeDtypeStruct((n, MLA_HEADS * MLA_V), BF16),
        scratch_shapes=[
            pltpu.VMEM((rows, QK_PAD), BF16),
            pltpu.VMEM((rows, tk), F32),
            pltpu.VMEM((rows, tk), BF16),
            pltpu.VMEM((rows, LANES), F32),
            pltpu.VMEM((rows, LANES), F32),
            pltpu.VMEM((rows, MLA_KV_LORA), F32),
        ],
        compiler_params=_cparams("arbitrary", "arbitrary"),
    )(q, kcat, wuv)


def _sample_attn_kernel(pt_ref, q_ref, ckvn_ref, krn_ref, wuv_ref, ckv_hbm, krt_hbm, o_ref,
                        kbuf, rbuf, sem, kc_ref, p_ref, qs_ref, knc_ref, knr_ref,
                        *, layer, n_pages, npg, nb, t):
    b = pl.program_id(0)
    ngroups = n_pages // npg
    rows = MLA_HEADS * t
    page = knc_ref.shape[0]

    def slot_of(g):
        return g % 2 if ngroups % 2 == 0 else lax.rem(b * ngroups + g, 2)

    def group_copies(bb, g, slot):
        copies = []
        for k in range(npg):
            pid = pt_ref[bb * n_pages + g * npg + k]
            copies.append(pltpu.make_async_copy(
                ckv_hbm.at[layer, pid], kbuf.at[slot, pl.ds(k * page, page), :], sem.at[slot]))
            copies.append(pltpu.make_async_copy(
                krt_hbm.at[layer, pid], rbuf.at[slot, :, pl.ds(k * page, page)], sem.at[slot]))
        return copies

    @pl.when(b == 0)
    def _():
        for c in group_copies(0, 0, slot_of(0)):
            c.start()

    for h in range(MLA_HEADS):
        qs_ref[h * t:(h + 1) * t, :] = q_ref[:, h * QK_PAD:(h + 1) * QK_PAD]
    qb = qs_ref[...].astype(BF16)
    ql = qb[:, :MLA_KV_LORA]
    qr = qb[:, MLA_KV_LORA:MLA_KV_LORA + MLA_ROPE]

    def softmax_step(s, m, l):
        m_new = jnp.maximum(m, jnp.max(s, axis=-1, keepdims=True))
        corr = jnp.exp(m - m_new)
        p = jnp.exp(s - m_new)
        return p.astype(BF16), corr, m_new, l * corr + jnp.sum(p, axis=-1, keepdims=True)

    m = jnp.full((rows, 1), NEG_INF, F32)
    l = jnp.zeros((rows, 1), F32)
    acc = jnp.zeros((rows, MLA_KV_LORA), F32)
    pending = None
    for g in range(ngroups):
        slot = slot_of(g)
        if g + 1 < ngroups:
            for c in group_copies(b, g + 1, 1 - slot):
                c.start()
        else:
            @pl.when(b + 1 < nb)
            def _():
                for c in group_copies(b + 1, 0, 1 - slot):
                    c.start()
        for c in group_copies(b, g, slot):
            c.wait()
        kc_ref[slot] = kbuf[slot].astype(BF16)
        s = _dot_nt(ql, kc_ref[slot]) + _dot(qr, rbuf[slot].astype(BF16))
        p, corr, m, l = softmax_step(s, m, l)
        p_ref[slot] = p
        if pending is not None:
            pslot, pcorr = pending
            acc = acc * pcorr + _dot(p_ref[pslot], kc_ref[pslot])
        pending = (slot, corr)
    pslot, pcorr = pending
    acc = acc * pcorr + _dot(p_ref[pslot], kc_ref[pslot])

    knc_ref[...] = jnp.zeros(knc_ref.shape, F32)
    knr_ref[...] = jnp.zeros(knr_ref.shape, F32)
    knc_ref[:t, :] = ckvn_ref[...]
    knr_ref[:t, :] = krn_ref[...]
    kc = knc_ref[...].astype(BF16)
    s = _dot_nt(ql, kc) + _dot_nt(qr, knr_ref[...].astype(BF16))
    tq = lax.rem(lax.broadcasted_iota(jnp.int32, (rows, page), 0), t)
    kidx = lax.broadcasted_iota(jnp.int32, (rows, page), 1)
    p, corr, m, l = softmax_step(jnp.where(kidx <= tq, s, NEG_INF), m, l)
    acc = acc * corr + _dot(p, kc)
    lat = acc / l
    for h in range(MLA_HEADS):
        o_ref[:, h * MLA_V:(h + 1) * MLA_V] = _dot(
            lat[h * t:(h + 1) * t].astype(BF16), wuv_ref[h]).astype(o_ref.dtype)


def _sample_attn(q, ckv_new, kr_new, wuv, cache_ckv, cache_krope, page_table, *, layer, nb, t):
    n_pages = page_table.shape[1]
    page = cache_ckv.shape[2]
    npg = min(n_pages, SAMPLE_PAGES_PER_GROUP)
    assert n_pages % npg == 0 and t <= page
    rows = MLA_HEADS * t
    pt_flat = page_table.reshape(-1)
    krope_t = jnp.swapaxes(cache_krope, 2, 3)
    in_specs = [
        pl.BlockSpec((t, MLA_HEADS * QK_PAD), lambda b, pt: (b, 0)),
        pl.BlockSpec((t, MLA_KV_LORA), lambda b, pt: (b, 0)),
        pl.BlockSpec((t, MLA_ROPE), lambda b, pt: (b, 0)),
        pl.BlockSpec(wuv.shape, lambda b, pt: (0, 0, 0)),
        pl.BlockSpec(memory_space=pl.ANY),
        pl.BlockSpec(memory_space=pl.ANY),
    ]
    grid_spec = pltpu.PrefetchScalarGridSpec(
        num_scalar_prefetch=1,
        grid=(nb,),
        in_specs=in_specs,
        out_specs=pl.BlockSpec((t, MLA_HEADS * MLA_V), lambda b, pt: (b, 0)),
        scratch_shapes=[
            pltpu.VMEM((2, npg * page, MLA_KV_LORA), F32),
            pltpu.VMEM((2, MLA_ROPE, npg * page), F32),
            pltpu.SemaphoreType.DMA((2,)),
            pltpu.VMEM((2, npg * page, MLA_KV_LORA), BF16),
            pltpu.VMEM((2, rows, npg * page), BF16),
            pltpu.VMEM((rows, QK_PAD), F32),
            pltpu.VMEM((page, MLA_KV_LORA), F32),
            pltpu.VMEM((page, MLA_ROPE), F32),
        ],
    )
    return pl.pallas_call(
        functools.partial(_sample_attn_kernel, layer=layer, n_pages=n_pages, npg=npg, nb=nb, t=t),
        name="sample_attn",
        grid_spec=grid_spec,
        out_shape=jax.ShapeDtypeStruct((nb * t, MLA_HEADS * MLA_V), F32),
        compiler_params=_cparams("arbitrary"),
    )(pt_flat, q, ckv_new, kr_new, wuv, cache_ckv, krope_t)


def _out_proj_kernel(ohg_ref, omla_ref, x_ref, w1_ref, w2_ref, g_ref, x1_ref, h2_ref):
    y = _dot(ohg_ref[...].astype(BF16), w1_ref[...]) + _dot(omla_ref[...].astype(BF16), w2_ref[...])
    x1 = x_ref[...] + y
    x1_ref[...] = x1
    h2_ref[...] = _rms_rows(x1, g_ref[...]).astype(BF16)


def _out_proj(ohg, omla, x, w1, w2, g):
    n, d = x.shape
    tm = _tile(n, 512)
    row = lambda i: (i, 0)
    fixed = lambda i: (0, 0)
    return pl.pallas_call(
        _out_proj_kernel,
        name="out_proj",
        grid=(n // tm,),
        in_specs=[
            pl.BlockSpec((tm, ohg.shape[1]), row),
            pl.BlockSpec((tm, omla.shape[1]), row),
            pl.BlockSpec((tm, d), row),
            pl.BlockSpec(w1.shape, fixed),
            pl.BlockSpec(w2.shape, fixed),
            pl.BlockSpec((1, d), fixed),
        ],
        out_specs=[pl.BlockSpec((tm, d), row), pl.BlockSpec((tm, d), row)],
        out_shape=[jax.ShapeDtypeStruct((n, d), F32), jax.ShapeDtypeStruct((n, d), BF16)],
        compiler_params=_cparams("arbitrary"),
    )(ohg, omla, x, w1, w2, g)


def _ffn_kernel(*refs, tm, nc, tiles_per_seq, t_sample):
    if t_sample:
        (h_ref, x_ref, wa_ref, wg_ref, wd_ref, cw_ref, cb_ref, sp_ref, o_ref, tail_ref,
         act_ref) = refs
    else:
        (h_ref, x_ref, wa_ref, wg_ref, wd_ref, cw_ref, cb_ref, o_ref, tail_ref,
         act_ref, halo_ref) = refs
    i = pl.program_id(0)
    c = pl.program_id(1)
    tc = wa_ref.shape[1]

    def up_and_activate():
        h = h_ref[...]
        a = _dot(h, wa_ref[...])
        gate = _dot(h, wg_ref[...])
        r1 = pltpu.roll(a, 1, 0)
        r2 = pltpu.roll(a, 2, 0)
        if t_sample:
            sp = sp_ref[...]
            tt = lax.rem(lax.broadcasted_iota(jnp.int32, (tm, tc), 0), t_sample)
            a1 = jnp.where(tt == 0, pltpu.roll(sp, tm - (t_sample - 1), 0), r1)
            a2 = jnp.where(tt < 2, pltpu.roll(sp, tm - (t_sample - 2), 0), r2)
            tail_ref[...] = a
        else:
            halo = jnp.where(lax.rem(i, tiles_per_seq) == 0, 0.0, halo_ref[c])
            row8 = lax.broadcasted_iota(jnp.int32, (SUBLANES, tc), 0)
            top1 = jnp.where(row8 < 1, pltpu.roll(halo, 1, 0), r1[:SUBLANES])
            top2 = jnp.where(row8 < 2, pltpu.roll(halo, 2, 0), r2[:SUBLANES])
            a1 = jnp.concatenate([top1, r1[SUBLANES:]], axis=0)
            a2 = jnp.concatenate([top2, r2[SUBLANES:]], axis=0)
            halo_ref[c] = a[tm - SUBLANES:]
            tail_ref[0] = a[tm - SUBLANES:]
        cw = cw_ref[...]
        conv = ((cb_ref[...] + cw[0:1] * a2) + cw[1:2] * a1) + cw[2:3] * a
        act_ref[lax.rem(c, 2)] = ((conv * jax.nn.sigmoid(conv)) * gate).astype(BF16)

    def down_previous():
        o_ref[...] += _dot(act_ref[lax.rem(c + 1, 2)], wd_ref[...])

    @pl.when(c == 0)
    def _():
        if not t_sample:
            @pl.when(i == 0)
            def _():
                halo_ref[...] = jnp.zeros(halo_ref.shape, F32)
        o_ref[...] = x_ref[...]
        up_and_activate()

    @pl.when(jnp.logical_and(c > 0, c < nc))
    def _():
        up_and_activate()
        down_previous()

    @pl.when(c == nc)
    def _():
        down_previous()


def _ffn(h2, x1, w_up, w_down, conv_w, conv_b, sp, *, t, t_sample):
    n, d = x1.shape
    dff = w_down.shape[0]
    tc = 512
    nc = dff // tc
    assert dff % tc == 0
    up = lambda c: jnp.minimum(c, nc - 1)
    if t_sample:
        tm = _tile(n, 512)
        assert tm % t_sample == 0
        tiles_per_seq = 1
        tail_shape = (n, dff)
        tail_spec = pl.BlockSpec((tm, tc), lambda i, c: (i, up(c)))
    else:
        tm = _tile(t, 512)
        tiles_per_seq = t // tm
        tail_shape = (n // tm, SUBLANES, dff)
        tail_spec = pl.BlockSpec((1, SUBLANES, tc), lambda i, c: (i, 0, up(c)))
    in_specs = [
        pl.BlockSpec((tm, d), lambda i, c: (i, 0)),
        pl.BlockSpec((tm, d), lambda i, c: (i, 0)),
        pl.BlockSpec((d, tc), lambda i, c: (0, up(c))),
        pl.BlockSpec((d, tc), lambda i, c: (0, nc + up(c))),
        pl.BlockSpec((tc, d), lambda i, c: (jnp.maximum(c - 1, 0), 0)),
        pl.BlockSpec((CONV_W, tc), lambda i, c: (0, up(c))),
        pl.BlockSpec((1, tc), lambda i, c: (0, up(c))),
    ]
    args = [h2, x1, w_up, w_up, w_down, conv_w, conv_b]
    scratch = [pltpu.VMEM((2, tm, tc), BF16)]
    if t_sample:
        in_specs.append(pl.BlockSpec((tm, tc), lambda i, c: (i, up(c))))
        args.append(sp)
    else:
        scratch.append(pltpu.VMEM((nc, SUBLANES, tc), F32))
    return pl.pallas_call(
        functools.partial(_ffn_kernel, tm=tm, nc=nc, tiles_per_seq=tiles_per_seq,
                          t_sample=t_sample),
        name="ffn",
        grid=(n // tm, nc + 1),
        in_specs=in_specs,
        out_specs=[pl.BlockSpec((tm, d), lambda i, c: (i, 0)), tail_spec],
        out_shape=[jax.ShapeDtypeStruct((n, d), F32), jax.ShapeDtypeStruct(tail_shape, F32)],
        scratch_shapes=scratch,
        compiler_params=_cparams("arbitrary", "arbitrary"),
    )(*args)


def _ple_kernel(x_ref, p_ref, gn_ref, wg_ref, wp_ref, fn_ref, y_ref, *, final):
    x = x_ref[...]
    hn = _rms_rows(x, gn_ref[...]).astype(BF16)
    gate = jax.nn.sigmoid(_dot(hn, wg_ref[...]))
    x3 = x + gate * _dot(p_ref[...].astype(BF16), wp_ref[...])
    y_ref[...] = _rms_rows(x3, fn_ref[...]) if final else x3


def _ple(x2, p, gn, wg, wp, fn, *, final):
    n, d = x2.shape
    tm = _tile(n, 512)
    row = lambda i: (i, 0)
    fixed = lambda i: (0, 0)
    return pl.pallas_call(
        functools.partial(_ple_kernel, final=final),
        name="ple",
        grid=(n // tm,),
        in_specs=[
            pl.BlockSpec((tm, d), row),
            pl.BlockSpec((tm, p.shape[1]), row),
            pl.BlockSpec((1, d), fixed),
            pl.BlockSpec(wg.shape, fixed),
            pl.BlockSpec(wp.shape, fixed),
            pl.BlockSpec((1, d), fixed),
        ],
        out_specs=pl.BlockSpec((tm, d), row),
        out_shape=jax.ShapeDtypeStruct((n, d), F32),
        compiler_params=_cparams("arbitrary"),
    )(x2, p, gn, wg, wp, fn)


def _rot_cols(w):
    half = MLA_ROPE // 2
    return jnp.concatenate([-w[..., half:], w[..., :half]], axis=-1)


def _rope_table(pos):
    half = MLA_ROPE // 2
    inv = 1.0 / (ROPE_THETA ** (jnp.arange(half, dtype=F32) / half))
    ang = pos.astype(F32)[:, None] * inv[None, :]
    c, s = jnp.cos(ang), jnp.sin(ang)
    return jnp.concatenate([c, c, s, s], axis=-1)


def _layer_weights(l, norm_mix, w_in, hg_onorm, mla_q_norm, w_q_b, mla_kv_norm, w_kv_b, w_out,
                   norm_ffn, w_up, conv_w, conv_b, w_down, norm_ple, w_ple_gate, w_ple_proj):
    d = w_in.shape[1]
    wi = w_in[l]
    kr_cols = wi[:, HG_COLS + MLA_Q_LORA + MLA_KV_LORA:]
    pad = MLA_COLS - (MLA_Q_LORA + MLA_KV_LORA + 2 * MLA_ROPE)
    w_in_ext = jnp.concatenate([wi, _rot_cols(kr_cols), jnp.zeros((d, pad), wi.dtype)],
                               axis=1).astype(BF16)
    wq = w_q_b[l].reshape(MLA_Q_LORA, MLA_HEADS, MLA_NOPE + MLA_ROPE)
    wq_ext = jnp.concatenate([wq, _rot_cols(wq[..., MLA_NOPE:])], axis=-1)
    wq_ext = wq_ext.reshape(MLA_Q_LORA, MLA_HEADS * Q_HEAD_COLS).astype(BF16)
    wkv = w_kv_b[l].reshape(MLA_KV_LORA, MLA_HEADS, MLA_NOPE + MLA_V)
    wuk = jnp.transpose(wkv[..., :MLA_NOPE], (1, 2, 0)).astype(BF16)
    wuv = jnp.transpose(wkv[..., MLA_NOPE:], (1, 0, 2)).astype(BF16)
    wo = w_out[l].astype(BF16)
    hgw = HG_HEADS * HG_DV
    return dict(
        norm_mix=norm_mix[l][None], w_in=w_in_ext, hg_onorm=hg_onorm[l][None],
        q_norm=mla_q_norm[l][None], wq=wq_ext, kv_norm=mla_kv_norm[l][None], wuk=wuk, wuv=wuv,
        wo_hg=wo[:hgw], wo_mla=wo[hgw:], norm_ffn=norm_ffn[l][None], w_up=w_up[l].astype(BF16),
        conv_w=conv_w[l], conv_b=conv_b[l][None], w_down=w_down[l].astype(BF16),
        norm_ple=norm_ple[l][None], w_ple_gate=w_ple_gate[l].astype(BF16),
        w_ple_proj=w_ple_proj[l].astype(BF16))


def _trunk(x, p_l, w, hg_lower, norm_final, cs, *, layer, nb, t, final, hg_s0, attend, sample):
    proj = _in_proj(x, w["norm_mix"], w["w_in"])
    ckv, kr, kcat, q = _mla_proj(proj, cs, w["q_norm"], w["kv_norm"], w["wq"], w["wuk"],
                                 F32 if sample else BF16)
    o_hg, hg_s = _hgrn(proj, hg_lower, w["hg_onorm"], hg_s0, layer=layer, nb=nb, t=t,
                       seq_per_step=min(nb, 16) if sample else 1,
                       out_dtype=F32 if sample else BF16)
    o_mla = attend(q, ckv, kr, kcat, w["wuv"])
    x1, h2 = _out_proj(o_hg, o_mla, x, w["wo_hg"], w["wo_mla"], w["norm_ffn"])
    return x1, h2, ckv, kr, hg_s


def kernel(x_prompt, x_sample, cache_ckv, cache_krope, state_hgrn, state_conv, page_table,
           p_prompt, p_sample, norm_mix, w_in, hg_lower, hg_onorm, mla_q_norm, w_q_b,
           mla_kv_norm, w_kv_b, w_out, norm_ffn, w_up, conv_w, conv_b, w_down, norm_ple,
           w_ple_gate, w_ple_proj, norm_final):
    depth = w_in.shape[0]
    bp, tp, d = x_prompt.shape
    bs, ts, _ = x_sample.shape
    dff = w_down.shape[1]
    cs_p = _rope_table(jnp.arange(tp, dtype=jnp.int32))
    cs_s = jnp.tile(_rope_table(PAST_LEN + jnp.arange(ts, dtype=jnp.int32)), (bs, 1))
    nf = norm_final[None]
    xp = x_prompt.reshape(bp * tp, d)
    xs = x_sample.reshape(bs * ts, d)
    outs_p, outs_s = [], []
    for l in range(depth):
        w = _layer_weights(l, norm_mix, w_in, hg_onorm, mla_q_norm, w_q_b, mla_kv_norm, w_kv_b,
                           w_out, norm_ffn, w_up, conv_w, conv_b, w_down, norm_ple, w_ple_gate,
                           w_ple_proj)
        final = l == depth - 1

        attend_p = lambda q, ckv, kr, kcat, wuv: _prompt_attn(q, kcat, wuv, nb=bp, t=tp)
        x1, h2, ckv_p, kr_p, hg_p = _trunk(xp, None, w, hg_lower, nf, cs_p, layer=l, nb=bp, t=tp,
                                           final=final, hg_s0=None, attend=attend_p, sample=False)
        x2, tail = _ffn(h2, x1, w["w_up"], w["w_down"], w["conv_w"], w["conv_b"], None,
                        t=tp, t_sample=0)
        cv_p = tail.reshape(bp, -1, SUBLANES, dff)[:, -1, SUBLANES - (CONV_W - 1):]
        xp = _ple(x2, p_prompt[l].reshape(bp * tp, -1), w["norm_ple"], w["w_ple_gate"],
                  w["w_ple_proj"], nf, final=final)
        outs_p.append((ckv_p.reshape(bp, tp, -1), kr_p.reshape(bp, tp, -1), hg_p, cv_p))

        attend_s = lambda q, ckv, kr, kcat, wuv: _sample_attn(
            q, ckv, kr, wuv, cache_ckv, cache_krope, page_table, layer=l, nb=bs, t=ts)
        x1, h2, ckv_s, kr_s, hg_s = _trunk(xs, None, w, hg_lower, nf, cs_s, layer=l, nb=bs, t=ts,
                                           final=final, hg_s0=state_hgrn[l], attend=attend_s,
                                           sample=True)
        sp = jnp.pad(state_conv[l], ((0, 0), (ts - (CONV_W - 1), 0), (0, 0))).reshape(bs * ts, dff)
        x2, a_full = _ffn(h2, x1, w["w_up"], w["w_down"], w["conv_w"], w["conv_b"], sp,
                          t=ts, t_sample=ts)
        cv_s = a_full.reshape(bs, ts, dff)[:, ts - (CONV_W - 1):]
        xs = _ple(x2, p_sample[l].reshape(bs * ts, -1), w["norm_ple"], w["w_ple_gate"],
                  w["w_ple_proj"], nf, final=final)
        outs_s.append((ckv_s.reshape(bs, ts, -1), kr_s.reshape(bs, ts, -1), hg_s, cv_s))

    stack = lambda outs, k: jnp.stack([o[k] for o in outs])
    return (xp.reshape(bp, tp, d), xs.reshape(bs, ts, d),
            stack(outs_p, 0), stack(outs_p, 1), stack(outs_p, 2), stack(outs_p, 3),
            stack(outs_s, 0), stack(outs_s, 1), stack(outs_s, 2), stack(outs_s, 3))
```

```python
import functools

import jax
import jax.numpy as jnp
from jax import lax
from jax.experimental import pallas as pl
from jax.experimental.pallas import tpu as pltpu

F32 = jnp.float32
BF16 = jnp.bfloat16

HG_HEADS = 8
HG_DK = 128
HG_DV = 128
HG_CHUNK = 64
HG_GROUP_ROWS = 256
IN_PROJ_COLS = 1024
FFN_CHUNK = 512
SAMPLE_PAGES_PER_GROUP = 32
ATTN_SOFTMAX_ROWS = 32
ATTN_SOFTMAX_UNROLL = 32
MLA_HEADS = 8
MLA_Q_LORA = 512
MLA_KV_LORA = 256
MLA_NOPE = 128
MLA_ROPE = 64
MLA_V = 128
MLA_SCALE = (MLA_NOPE + MLA_ROPE) ** -0.5
ROPE_THETA = 10000.0
PAST_LEN = 16384
CONV_W = 3
EPS = 1e-6
NEG_INF = -1e30

HG_COLS = 4 * HG_HEADS * HG_DK
MLA_COLS = 1024
Q_HEAD_COLS = 256
QK_PAD = 384
LANES = 128
SUBLANES = 8

VMEM_LIMIT = 48 * 1024 * 1024

_NT = (((1,), (1,)), ((), ()))
_TN = (((0,), (0,)), ((), ()))


def _cparams(*sem):
    return pltpu.CompilerParams(dimension_semantics=sem, vmem_limit_bytes=VMEM_LIMIT)


def _tile(n, pref):
    t = min(n, pref)
    assert n % t == 0, (n, pref)
    return t


def _rms_rows(x, g):
    ms = jnp.mean(x * x, axis=-1, keepdims=True)
    return (x * lax.rsqrt(ms + EPS)) * g


def _dot(a, b):
    return jnp.dot(a, b, preferred_element_type=F32)


def _dot_nt(a, b):
    return lax.dot_general(a, b, _NT, preferred_element_type=F32)


def _in_proj_kernel(x_ref, g_ref, w_ref, o_ref, h_ref):
    @pl.when(pl.program_id(1) == 0)
    def _():
        h_ref[...] = _rms_rows(x_ref[...], g_ref[...]).astype(BF16)

    o_ref[...] = _dot(h_ref[...], w_ref[...])


def _col_blocks(w, tn):
    d, cols = w.shape
    return jnp.transpose(w.reshape(d, cols // tn, tn), (1, 0, 2))


def _in_proj(x, g, w):
    n, d = x.shape
    nblk, _, tn = w.shape
    tm = _tile(n, 1024)
    return pl.pallas_call(
        _in_proj_kernel,
        name="in_proj",
        grid=(n // tm, nblk),
        in_specs=[
            pl.BlockSpec((tm, d), lambda i, j: (i, 0)),
            pl.BlockSpec((1, d), lambda i, j: (0, 0)),
            pl.BlockSpec((None, d, tn), lambda i, j: (j, 0, 0)),
        ],
        out_specs=pl.BlockSpec((tm, tn), lambda i, j: (i, j)),
        out_shape=jax.ShapeDtypeStruct((n, nblk * tn), F32),
        scratch_shapes=[pltpu.VMEM((tm, d), BF16)],
        compiler_params=_cparams("arbitrary", "arbitrary"),
    )(x, g, w)


def _mla_proj_kernel(p_ref, cs_ref, qn_ref, kvn_ref, wq_ref, wuk_ref,
                     ckv_ref, kr_ref, kcat_ref, q_ref):
    p = p_ref[...]
    cs = cs_ref[...]
    tm = p.shape[0]
    lane = lax.broadcasted_iota(jnp.int32, (tm, LANES), 1)
    cq = _rms_rows(p[:, :MLA_Q_LORA], qn_ref[...]).astype(BF16)
    ckv = _rms_rows(p[:, MLA_Q_LORA:MLA_Q_LORA + MLA_KV_LORA], kvn_ref[...])
    ckv_ref[...] = ckv
    kk = p[:, 768:896] * cs
    kro = kk + pltpu.roll(kk, MLA_ROPE, 1)
    kr_ref[...] = kro[:, :MLA_ROPE]
    kcat_ref[:, :MLA_KV_LORA] = ckv.astype(kcat_ref.dtype)
    kcat_ref[:, MLA_KV_LORA:] = jnp.where(lane < MLA_ROPE, kro, 0.0).astype(kcat_ref.dtype)
    q = _dot(cq, wq_ref[...])
    for h in range(MLA_HEADS):
        qh = q[:, h * Q_HEAD_COLS:(h + 1) * Q_HEAD_COLS]
        qlat = _dot(qh[:, :MLA_NOPE].astype(BF16), wuk_ref[h]) * MLA_SCALE
        rr = qh[:, MLA_NOPE:] * cs
        rope = (rr + pltpu.roll(rr, MLA_ROPE, 1)) * MLA_SCALE
        q_ref[:, h * QK_PAD:h * QK_PAD + MLA_KV_LORA] = qlat.astype(q_ref.dtype)
        q_ref[:, h * QK_PAD + MLA_KV_LORA:(h + 1) * QK_PAD] = (
            jnp.where(lane < MLA_ROPE, rope, 0.0).astype(q_ref.dtype))


def _mla_proj(proj, cs, qn, kvn, wq, wuk, q_dtype):
    n = proj.shape[0]
    tm = _tile(n, 256)
    cs_blocks = cs.shape[0] // tm
    mla_blk = HG_COLS // MLA_COLS
    return pl.pallas_call(
        _mla_proj_kernel,
        name="mla_proj",
        grid=(n // tm,),
        in_specs=[
            pl.BlockSpec((tm, MLA_COLS), lambda i: (i, mla_blk)),
            pl.BlockSpec((tm, LANES), lambda i: (i % cs_blocks, 0)),
            pl.BlockSpec((1, MLA_Q_LORA), lambda i: (0, 0)),
            pl.BlockSpec((1, MLA_KV_LORA), lambda i: (0, 0)),
            pl.BlockSpec(wq.shape, lambda i: (0, 0)),
            pl.BlockSpec(wuk.shape, lambda i: (0, 0, 0)),
        ],
        out_specs=[
            pl.BlockSpec((tm, MLA_KV_LORA), lambda i: (i, 0)),
            pl.BlockSpec((tm, MLA_ROPE), lambda i: (i, 0)),
            pl.BlockSpec((tm, QK_PAD), lambda i: (i, 0)),
            pl.BlockSpec((tm, MLA_HEADS * QK_PAD), lambda i: (i, 0)),
        ],
        out_shape=[
            jax.ShapeDtypeStruct((n, MLA_KV_LORA), F32),
            jax.ShapeDtypeStruct((n, MLA_ROPE), F32),
            jax.ShapeDtypeStruct((n, QK_PAD), BF16),
            jax.ShapeDtypeStruct((n, MLA_HEADS * QK_PAD), q_dtype),
        ],
        compiler_params=_cparams("arbitrary"),
    )(proj, cs, qn, kvn, wq, wuk)


def _hgrn_kernel(*refs, layer, nseq, nchunk, chunk, has_s0):
    if has_s0:
        hl_ref, q_ref, f_ref, i_ref, g_ref, on_ref, s0_ref, o_ref, sout_ref = refs
    else:
        hl_ref, q_ref, f_ref, i_ref, g_ref, on_ref, o_ref, sout_ref = refs
        s0_ref = None
    hl = hl_ref[...]
    e = jnp.exp(hl - jnp.max(hl, axis=0, keepdims=True))
    lb = jnp.sum(e[:layer + 1], axis=0, keepdims=True) / jnp.sum(e, axis=0, keepdims=True)
    onorm = on_ref[...]
    sup = min(HG_GROUP_ROWS // chunk, nseq * nchunk)
    rg = sup * chunk
    ngroups = (nseq * nchunk) // sup
    assert (nseq * nchunk) % sup == 0 and (nseq == 1 or nchunk == 1)
    rix = lax.broadcasted_iota(jnp.int32, (rg, rg), 0)
    cix = lax.broadcasted_iota(jnp.int32, (rg, rg), 1)
    same = (rix // chunk) == (cix // chunk)
    causal = jnp.logical_and(same, rix >= cix)
    scan_lhs = jnp.concatenate([causal, same], axis=0).astype(BF16)

    def group_body(s, st):
        r0 = pl.multiple_of(s * rg, rg)
        qr = q_ref[pl.ds(r0, rg), :]
        fx = f_ref[pl.ds(r0, rg), :]
        v = i_ref[pl.ds(r0, rg), :]
        gr = g_ref[pl.ds(r0, rg), :]
        sg = jax.nn.sigmoid(fx)
        f = lb + (1.0 - lb) * sg
        k = (1.0 - lb) * (1.0 - sg)
        logf = jnp.log(f)
        hi = logf.astype(BF16)
        r1 = logf - hi.astype(F32)
        mid = r1.astype(BF16)
        lo = (r1 - mid.astype(F32)).astype(BF16)
        parts = _dot(scan_lhs, jnp.concatenate([hi, mid, lo], axis=1))
        sums = (parts[:, :HG_DK] + parts[:, HG_DK:2 * HG_DK]) + parts[:, 2 * HG_DK:]
        cum = sums[:rg]
        tot = sums[rg:]
        qd = (qr * jax.nn.sigmoid(qr)) * jnp.exp(cum)
        ki = (k * jnp.exp(-cum)).astype(BF16)
        ke = k * jnp.exp(tot - cum)
        dec = jnp.exp(tot)
        a = jnp.where(causal, _dot_nt(qd.astype(BF16), ki), 0.0)
        o_intra = _dot(a.astype(BF16), v.astype(BF16))
        o_state = []
        for c in range(sup):
            cs = slice(c * chunk, (c + 1) * chunk)
            if nchunk == 1:
                g = s * sup + c
                st = s0_ref[g, 0].T if has_s0 else jnp.zeros((HG_DV, HG_DK), F32)
            qd_c = qd[cs].astype(BF16)
            o_state.append(_dot_nt(qd_c, st.astype(BF16)))
            upd = lax.dot_general(v[cs].astype(BF16), ke[cs].astype(BF16), _TN,
                                  preferred_element_type=F32)
            st = st * dec[c * chunk:c * chunk + 1] + upd
            if nchunk == 1:
                sout_ref[g, 0] = st.T
        o = o_intra + jnp.concatenate(o_state, axis=0)
        out = _rms_rows(o, onorm) * jax.nn.sigmoid(gr)
        o_ref[pl.ds(r0, rg), :] = out.astype(o_ref.dtype)
        return st

    if nchunk == 1:
        lax.fori_loop(0, ngroups, group_body, jnp.zeros((HG_DV, HG_DK), F32))
    else:
        st0 = s0_ref[0, 0].T if has_s0 else jnp.zeros((HG_DV, HG_DK), F32)
        sout_ref[0, 0] = lax.fori_loop(0, ngroups, group_body, st0).T


def _hgrn(proj, hg_lower, onorm, s0, *, layer, nb, t, seq_per_step, out_dtype):
    n = proj.shape[0]
    assert n == nb * t
    chunk = min(t, HG_CHUNK)
    assert t % chunk == 0 and nb % seq_per_step == 0
    rows = seq_per_step * t
    nl = hg_lower.shape[0]

    def col(seg):
        return pl.BlockSpec((rows, HG_DK), lambda i, h: (i, seg * HG_HEADS + h))

    in_specs = [
        pl.BlockSpec((nl, HG_DK), lambda i, h: (0, h)),
        col(0), col(1), col(2), col(3),
        pl.BlockSpec((1, HG_DV), lambda i, h: (0, h)),
    ]
    args = [hg_lower, proj, proj, proj, proj, onorm]
    state_spec = pl.BlockSpec((seq_per_step, 1, HG_DK, HG_DV), lambda i, h: (i, h, 0, 0))
    if s0 is not None:
        in_specs.append(state_spec)
        args.append(s0)
    kern = functools.partial(_hgrn_kernel, layer=layer, nseq=seq_per_step, nchunk=t // chunk,
                             chunk=chunk, has_s0=s0 is not None)
    return pl.pallas_call(
        kern,
        name="hgrn",
        grid=(nb // seq_per_step, HG_HEADS),
        in_specs=in_specs,
        out_specs=[
            pl.BlockSpec((rows, HG_DV), lambda i, h: (i, h)),
            state_spec,
        ],
        out_shape=[
            jax.ShapeDtypeStruct((n, HG_HEADS * HG_DV), out_dtype),
            jax.ShapeDtypeStruct((nb, HG_HEADS, HG_DK, HG_DV), F32),
        ],
        compiler_params=_cparams("arbitrary", "arbitrary"),
    )(*args)


def _prompt_attn_kernel(q_ref, k_ref, wuv_ref, o_ref, qs_ref, s_ref, p_ref, m_ref, l_ref, acc_ref,
                        *, tq, tk):
    qi = pl.program_id(1)
    rows = MLA_HEADS * tq
    nlg = tk // LANES
    rb = ATTN_SOFTMAX_ROWS
    for h in range(MLA_HEADS):
        qs_ref[h * tq:(h + 1) * tq, :] = q_ref[:, h * QK_PAD:(h + 1) * QK_PAD]
    m_ref[...] = jnp.full((rows, LANES), NEG_INF, F32)
    l_ref[...] = jnp.zeros((rows, LANES), F32)
    acc_ref[...] = jnp.zeros((rows, MLA_KV_LORA), F32)
    nfull = (qi * tq) // tk

    def softmax_rows(r, limit):
        rs = pl.ds(pl.multiple_of(r * rb, rb), rb)
        sb = [s_ref[rs, i * LANES:(i + 1) * LANES] for i in range(nlg)]
        if limit is not None:
            tok = lax.rem(r * rb + lax.broadcasted_iota(jnp.int32, (rb, LANES), 0), tq)
            lane = lax.broadcasted_iota(jnp.int32, (rb, LANES), 1)
            sb = [jnp.where(lane + i * LANES - tok <= limit, sb[i], NEG_INF) for i in range(nlg)]
        mx = functools.reduce(jnp.maximum, sb)
        m_prev = m_ref[rs, :]
        m_new = jnp.maximum(m_prev, jnp.max(mx, axis=-1, keepdims=True))
        corr = jnp.exp(m_prev - m_new)
        ps = [jnp.exp(x - m_new) for x in sb]
        l_ref[rs, :] = l_ref[rs, :] * corr + jnp.sum(functools.reduce(jnp.add, ps), axis=-1,
                                                     keepdims=True)
        m_ref[rs, :] = m_new
        for i in range(nlg):
            p_ref[rs, i * LANES:(i + 1) * LANES] = ps[i].astype(BF16)
        for i in range(MLA_KV_LORA // LANES):
            acc_ref[rs, i * LANES:(i + 1) * LANES] = acc_ref[rs, i * LANES:(i + 1) * LANES] * corr

    def block(j, masked):
        kb = k_ref[pl.ds(pl.multiple_of(j * tk, tk), tk), :]
        hk = tk // 2
        s_ref[:, :hk] = _dot_nt(qs_ref[...], kb[:hk])
        s_ref[:, hk:] = _dot_nt(qs_ref[...], kb[hk:])
        limit = qi * tq - j * tk if masked else None

        def sweep(r, carry):
            softmax_rows(r, limit)
            return carry

        lax.fori_loop(0, rows // rb, sweep, 0, unroll=ATTN_SOFTMAX_UNROLL)
        hr = rows // 2
        acc_ref[:hr, :] += _dot(p_ref[:hr, :], kb[:, :MLA_KV_LORA])
        acc_ref[hr:, :] += _dot(p_ref[hr:, :], kb[:, :MLA_KV_LORA])

    def body(j, carry):
        block(j, False)
        return carry

    lax.fori_loop(0, nfull, body, 0)
    block(nfull, True)
    for h in range(MLA_HEADS):
        hs = slice(h * tq, (h + 1) * tq)
        inv = 1.0 / l_ref[hs, :]
        lat = jnp.concatenate([acc_ref[hs, i * LANES:(i + 1) * LANES] * inv
                               for i in range(MLA_KV_LORA // LANES)], axis=1)
        o_ref[:, h * MLA_V:(h + 1) * MLA_V] = _dot(lat.astype(BF16), wuv_ref[h]).astype(o_ref.dtype)


def _prompt_attn(q, kcat, wuv, *, nb, t):
    n = q.shape[0]
    tq = _tile(t, 128)
    tk = _tile(t, 512)
    assert tk % tq == 0
    nq = t // tq
    rows = MLA_HEADS * tq
    return pl.pallas_call(
        functools.partial(_prompt_attn_kernel, tq=tq, tk=tk),
        name="prompt_attn",
        grid=(nb, nq),
        in_specs=[
            pl.BlockSpec((tq, MLA_HEADS * QK_PAD), lambda b, i: (b * nq + i, 0)),
            pl.BlockSpec((t, QK_PAD), lambda b, i: (b, 0)),
            pl.BlockSpec(wuv.shape, lambda b, i: (0, 0, 0)),
        ],
        out_specs=pl.BlockSpec((tq, MLA_HEADS * MLA_V), lambda b, i: (b * nq + i, 0)),
        out_shape=jax.ShapeDtypeStruct((n, MLA_HEADS * MLA_V), BF16),
        scratch_shapes=[
            pltpu.VMEM((rows, QK_PAD), BF16),
            pltpu.VMEM((rows, tk), F32),
            pltpu.VMEM((rows, tk), BF16),
            pltpu.VMEM((rows, LANES), F32),
            pltpu.VMEM((rows, LANES), F32),
            pltpu.VMEM((rows, MLA_KV_LORA), F32),
        ],
        compiler_params=_cparams("arbitrary", "arbitrary"),
    )(q, kcat, wuv)


def _sample_attn_kernel(pt_ref, q_ref, ckvn_ref, krn_ref, wuv_ref, ckv_hbm, krt_hbm, o_ref,
                        kbuf, rbuf, sem, kc_ref, p_ref, qs_ref, knc_ref, knr_ref,
                        *, layer, n_pages, npg, nb, t):
    b = pl.program_id(0)
    ngroups = n_pages // npg
    rows = MLA_HEADS * t
    page = knc_ref.shape[0]

    def slot_of(g):
        return g % 2 if ngroups % 2 == 0 else lax.rem(b * ngroups + g, 2)

    def group_copies(bb, g, slot):
        copies = []
        for k in range(npg):
            pid = pt_ref[bb * n_pages + g * npg + k]
            copies.append(pltpu.make_async_copy(
                ckv_hbm.at[layer, pid], kbuf.at[slot, pl.ds(k * page, page), :], sem.at[slot]))
            copies.append(pltpu.make_async_copy(
                krt_hbm.at[layer, pid], rbuf.at[slot, k], sem.at[slot]))
        return copies

    def start_group(bb, g, slot):
        for n, c in enumerate(group_copies(bb, g, slot)):
            c.start(priority=(n // 2) % 2)

    @pl.when(b == 0)
    def _():
        start_group(0, 0, slot_of(0))

    for h in range(MLA_HEADS):
        qs_ref[h * t:(h + 1) * t, :] = q_ref[:, h * QK_PAD:(h + 1) * QK_PAD]
    qb = qs_ref[...].astype(BF16)
    ql = qb[:, :MLA_KV_LORA]
    qr = qb[:, MLA_KV_LORA:MLA_KV_LORA + MLA_ROPE]

    def softmax_step(s, m, l):
        m_new = jnp.maximum(m, jnp.max(s, axis=-1, keepdims=True))
        corr = jnp.exp(m - m_new)
        p = jnp.exp(s - m_new)
        return p.astype(BF16), corr, m_new, l * corr + jnp.sum(p, axis=-1, keepdims=True)

    m = jnp.full((rows, 1), NEG_INF, F32)
    l = jnp.zeros((rows, 1), F32)
    acc = jnp.zeros((rows, MLA_KV_LORA), F32)
    pending = None
    for g in range(ngroups):
        slot = slot_of(g)
        if g + 1 < ngroups:
            start_group(b, g + 1, 1 - slot)
        else:
            @pl.when(b + 1 < nb)
            def _():
                start_group(b + 1, 0, 1 - slot)
        for c in group_copies(b, g, slot):
            c.wait()
        kc_ref[slot] = kbuf[slot].astype(BF16)
        s_rope = jnp.concatenate(
            [_dot(qr, rbuf[slot, k].astype(BF16)) for k in range(npg)], axis=1)
        s = _dot_nt(ql, kc_ref[slot]) + s_rope
        p, corr, m, l = softmax_step(s, m, l)
        p_ref[slot] = p
        if pending is not None:
            pslot, pcorr = pending
            acc = acc * pcorr + _dot(p_ref[pslot], kc_ref[pslot])
        pending = (slot, corr)
    pslot, pcorr = pending
    acc = acc * pcorr + _dot(p_ref[pslot], kc_ref[pslot])

    knc_ref[...] = jnp.zeros(knc_ref.shape, F32)
    knr_ref[...] = jnp.zeros(knr_ref.shape, F32)
    knc_ref[:t, :] = ckvn_ref[...]
    knr_ref[:t, :] = krn_ref[...]
    kc = knc_ref[...].astype(BF16)
    s = _dot_nt(ql, kc) + _dot_nt(qr, knr_ref[...].astype(BF16))
    tq = lax.rem(lax.broadcasted_iota(jnp.int32, (rows, page), 0), t)
    kidx = lax.broadcasted_iota(jnp.int32, (rows, page), 1)
    p, corr, m, l = softmax_step(jnp.where(kidx <= tq, s, NEG_INF), m, l)
    acc = acc * corr + _dot(p, kc)
    lat = acc / l
    for h in range(MLA_HEADS):
        o_ref[:, h * MLA_V:(h + 1) * MLA_V] = _dot(
            lat[h * t:(h + 1) * t].astype(BF16), wuv_ref[h]).astype(o_ref.dtype)


def _sample_attn(q, ckv_new, kr_new, wuv, cache_ckv, cache_krope, page_table, *, layer, nb, t):
    n_pages = page_table.shape[1]
    page = cache_ckv.shape[2]
    npg = min(n_pages, SAMPLE_PAGES_PER_GROUP)
    assert n_pages % npg == 0 and t <= page
    rows = MLA_HEADS * t
    pt_flat = page_table.reshape(-1)
    krope_t = jnp.swapaxes(cache_krope, 2, 3)
    in_specs = [
        pl.BlockSpec((t, MLA_HEADS * QK_PAD), lambda b, pt: (b, 0)),
        pl.BlockSpec((t, MLA_KV_LORA), lambda b, pt: (b, 0)),
        pl.BlockSpec((t, MLA_ROPE), lambda b, pt: (b, 0)),
        pl.BlockSpec(wuv.shape, lambda b, pt: (0, 0, 0)),
        pl.BlockSpec(memory_space=pl.ANY),
        pl.BlockSpec(memory_space=pl.ANY),
    ]
    grid_spec = pltpu.PrefetchScalarGridSpec(
        num_scalar_prefetch=1,
        grid=(nb,),
        in_specs=in_specs,
        out_specs=pl.BlockSpec((t, MLA_HEADS * MLA_V), lambda b, pt: (b, 0)),
        scratch_shapes=[
            pltpu.VMEM((2, npg * page, MLA_KV_LORA), F32),
            pltpu.VMEM((2, npg, MLA_ROPE, page), F32),
            pltpu.SemaphoreType.DMA((2,)),
            pltpu.VMEM((2, npg * page, MLA_KV_LORA), BF16),
            pltpu.VMEM((2, rows, npg * page), BF16),
            pltpu.VMEM((rows, QK_PAD), F32),
            pltpu.VMEM((page, MLA_KV_LORA), F32),
            pltpu.VMEM((page, MLA_ROPE), F32),
        ],
    )
    return pl.pallas_call(
        functools.partial(_sample_attn_kernel, layer=layer, n_pages=n_pages, npg=npg, nb=nb, t=t),
        name="sample_attn",
        grid_spec=grid_spec,
        out_shape=jax.ShapeDtypeStruct((nb * t, MLA_HEADS * MLA_V), F32),
        compiler_params=_cparams("arbitrary"),
    )(pt_flat, q, ckv_new, kr_new, wuv, cache_ckv, krope_t)


def _out_proj_kernel(ohg_ref, omla_ref, x_ref, w1_ref, w2_ref, g_ref, x1_ref, h2_ref):
    y = _dot(ohg_ref[...].astype(BF16), w1_ref[...]) + _dot(omla_ref[...].astype(BF16), w2_ref[...])
    x1 = x_ref[...] + y
    x1_ref[...] = x1
    h2_ref[...] = _rms_rows(x1, g_ref[...]).astype(BF16)


def _out_proj(ohg, omla, x, w1, w2, g):
    n, d = x.shape
    tm = _tile(n, 512)
    row = lambda i: (i, 0)
    fixed = lambda i: (0, 0)
    return pl.pallas_call(
        _out_proj_kernel,
        name="out_proj",
        grid=(n // tm,),
        in_specs=[
            pl.BlockSpec((tm, ohg.shape[1]), row),
            pl.BlockSpec((tm, omla.shape[1]), row),
            pl.BlockSpec((tm, d), row),
            pl.BlockSpec(w1.shape, fixed),
            pl.BlockSpec(w2.shape, fixed),
            pl.BlockSpec((1, d), fixed),
        ],
        out_specs=[pl.BlockSpec((tm, d), row), pl.BlockSpec((tm, d), row)],
        out_shape=[jax.ShapeDtypeStruct((n, d), F32), jax.ShapeDtypeStruct((n, d), BF16)],
        compiler_params=_cparams("arbitrary"),
    )(ohg, omla, x, w1, w2, g)


def _ffn_kernel(*refs, tm, nc, tiles_per_seq, t_sample):
    if t_sample:
        (h_ref, x_ref, wa_ref, wg_ref, wd_ref, cw_ref, cb_ref, sp_ref, o_ref, tail_ref,
         act_ref) = refs
    else:
        (h_ref, x_ref, wa_ref, wg_ref, wd_ref, cw_ref, cb_ref, o_ref, tail_ref,
         act_ref, halo_ref) = refs
    i = pl.program_id(0)
    c = pl.program_id(1)
    tc = wa_ref.shape[1]

    def up_and_activate():
        h = h_ref[...]
        a = _dot(h, wa_ref[...])
        gate = _dot(h, wg_ref[...])
        r1 = pltpu.roll(a, 1, 0)
        r2 = pltpu.roll(a, 2, 0)
        if t_sample:
            sp = sp_ref[...]
            tt = lax.rem(lax.broadcasted_iota(jnp.int32, (tm, tc), 0), t_sample)
            a1 = jnp.where(tt == 0, pltpu.roll(sp, tm - (t_sample - 1), 0), r1)
            a2 = jnp.where(tt < 2, pltpu.roll(sp, tm - (t_sample - 2), 0), r2)
            tail_ref[...] = a
        else:
            halo = jnp.where(lax.rem(i, tiles_per_seq) == 0, 0.0, halo_ref[c])
            row8 = lax.broadcasted_iota(jnp.int32, (SUBLANES, tc), 0)
            top1 = jnp.where(row8 < 1, pltpu.roll(halo, 1, 0), r1[:SUBLANES])
            top2 = jnp.where(row8 < 2, pltpu.roll(halo, 2, 0), r2[:SUBLANES])
            a1 = jnp.concatenate([top1, r1[SUBLANES:]], axis=0)
            a2 = jnp.concatenate([top2, r2[SUBLANES:]], axis=0)
            halo_ref[c] = a[tm - SUBLANES:]
            tail_ref[0] = a[tm - SUBLANES:]
        cw = cw_ref[...]
        conv = ((cb_ref[...] + cw[0:1] * a2) + cw[1:2] * a1) + cw[2:3] * a
        act_ref[lax.rem(c, 2)] = ((conv * jax.nn.sigmoid(conv)) * gate).astype(BF16)

    def down_previous():
        o_ref[...] += _dot(act_ref[lax.rem(c + 1, 2)], wd_ref[...])

    @pl.when(c == 0)
    def _():
        if not t_sample:
            @pl.when(i == 0)
            def _():
                halo_ref[...] = jnp.zeros(halo_ref.shape, F32)
        o_ref[...] = x_ref[...]
        up_and_activate()

    @pl.when(jnp.logical_and(c > 0, c < nc))
    def _():
        up_and_activate()
        down_previous()

    @pl.when(c == nc)
    def _():
        down_previous()


def _ffn(h2, x1, w_up, w_down, conv_w, conv_b, sp, *, t, t_sample):
    n, d = x1.shape
    dff = w_down.shape[0]
    tc = w_up.shape[2]
    nc = dff // tc
    assert w_up.shape[0] == 2 * nc
    up = lambda c: jnp.minimum(c, nc - 1)
    if t_sample:
        tm = _tile(n, 512)
        assert tm % t_sample == 0
        tiles_per_seq = 1
        tail_shape = (n, dff)
        tail_spec = pl.BlockSpec((tm, tc), lambda i, c: (i, up(c)))
    else:
        tm = _tile(t, 512)
        tiles_per_seq = t // tm
        tail_shape = (n // tm, SUBLANES, dff)
        tail_spec = pl.BlockSpec((1, SUBLANES, tc), lambda i, c: (i, 0, up(c)))
    in_specs = [
        pl.BlockSpec((tm, d), lambda i, c: (i, 0)),
        pl.BlockSpec((tm, d), lambda i, c: (i, 0)),
        pl.BlockSpec((None, d, tc), lambda i, c: (up(c), 0, 0)),
        pl.BlockSpec((None, d, tc), lambda i, c: (nc + up(c), 0, 0)),
        pl.BlockSpec((tc, d), lambda i, c: (jnp.maximum(c - 1, 0), 0)),
        pl.BlockSpec((CONV_W, tc), lambda i, c: (0, up(c))),
        pl.BlockSpec((1, tc), lambda i, c: (0, up(c))),
    ]
    args = [h2, x1, w_up, w_up, w_down, conv_w, conv_b]
    scratch = [pltpu.VMEM((2, tm, tc), BF16)]
    if t_sample:
        in_specs.append(pl.BlockSpec((tm, tc), lambda i, c: (i, up(c))))
        args.append(sp)
    else:
        scratch.append(pltpu.VMEM((nc, SUBLANES, tc), F32))
    return pl.pallas_call(
        functools.partial(_ffn_kernel, tm=tm, nc=nc, tiles_per_seq=tiles_per_seq,
                          t_sample=t_sample),
        name="ffn",
        grid=(n // tm, nc + 1),
        in_specs=in_specs,
        out_specs=[pl.BlockSpec((tm, d), lambda i, c: (i, 0)), tail_spec],
        out_shape=[jax.ShapeDtypeStruct((n, d), F32), jax.ShapeDtypeStruct(tail_shape, F32)],
        scratch_shapes=scratch,
        compiler_params=_cparams("arbitrary", "arbitrary"),
    )(*args)


def _ple_kernel(x_ref, p_ref, gn_ref, wg_ref, wp_ref, fn_ref, y_ref, *, final):
    x = x_ref[...]
    hn = _rms_rows(x, gn_ref[...]).astype(BF16)
    gate = jax.nn.sigmoid(_dot(hn, wg_ref[...]))
    x3 = x + gate * _dot(p_ref[...].astype(BF16), wp_ref[...])
    y_ref[...] = _rms_rows(x3, fn_ref[...]) if final else x3


def _ple(x2, p, gn, wg, wp, fn, *, final):
    n, d = x2.shape
    tm = _tile(n, 512)
    row = lambda i: (i, 0)
    fixed = lambda i: (0, 0)
    return pl.pallas_call(
        functools.partial(_ple_kernel, final=final),
        name="ple",
        grid=(n // tm,),
        in_specs=[
            pl.BlockSpec((tm, d), row),
            pl.BlockSpec((tm, p.shape[1]), row),
            pl.BlockSpec((1, d), fixed),
            pl.BlockSpec(wg.shape, fixed),
            pl.BlockSpec(wp.shape, fixed),
            pl.BlockSpec((1, d), fixed),
        ],
        out_specs=pl.BlockSpec((tm, d), row),
        out_shape=jax.ShapeDtypeStruct((n, d), F32),
        compiler_params=_cparams("arbitrary"),
    )(x2, p, gn, wg, wp, fn)


def _rot_cols(w):
    half = MLA_ROPE // 2
    return jnp.concatenate([-w[..., half:], w[..., :half]], axis=-1)


def _rope_table(pos):
    half = MLA_ROPE // 2
    inv = 1.0 / (ROPE_THETA ** (jnp.arange(half, dtype=F32) / half))
    ang = pos.astype(F32)[:, None] * inv[None, :]
    c, s = jnp.cos(ang), jnp.sin(ang)
    return jnp.concatenate([c, c, s, s], axis=-1)


def _layer_weights(l, norm_mix, w_in, hg_onorm, mla_q_norm, w_q_b, mla_kv_norm, w_kv_b, w_out,
                   norm_ffn, w_up, conv_w, conv_b, w_down, norm_ple, w_ple_gate, w_ple_proj):
    d = w_in.shape[1]
    wi = w_in[l]
    kr_cols = wi[:, HG_COLS + MLA_Q_LORA + MLA_KV_LORA:]
    pad = MLA_COLS - (MLA_Q_LORA + MLA_KV_LORA + 2 * MLA_ROPE)
    w_in_ext = jnp.concatenate([wi, _rot_cols(kr_cols), jnp.zeros((d, pad), wi.dtype)],
                               axis=1).astype(BF16)
    w_in_ext = _col_blocks(w_in_ext, IN_PROJ_COLS)
    wq = w_q_b[l].reshape(MLA_Q_LORA, MLA_HEADS, MLA_NOPE + MLA_ROPE)
    wq_ext = jnp.concatenate([wq, _rot_cols(wq[..., MLA_NOPE:])], axis=-1)
    wq_ext = wq_ext.reshape(MLA_Q_LORA, MLA_HEADS * Q_HEAD_COLS).astype(BF16)
    wkv = w_kv_b[l].reshape(MLA_KV_LORA, MLA_HEADS, MLA_NOPE + MLA_V)
    wuk = jnp.transpose(wkv[..., :MLA_NOPE], (1, 2, 0)).astype(BF16)
    wuv = jnp.transpose(wkv[..., MLA_NOPE:], (1, 0, 2)).astype(BF16)
    wo = w_out[l].astype(BF16)
    hgw = HG_HEADS * HG_DV
    return dict(
        norm_mix=norm_mix[l][None], w_in=w_in_ext, hg_onorm=hg_onorm[l][None],
        q_norm=mla_q_norm[l][None], wq=wq_ext, kv_norm=mla_kv_norm[l][None], wuk=wuk, wuv=wuv,
        wo_hg=wo[:hgw], wo_mla=wo[hgw:], norm_ffn=norm_ffn[l][None],
        w_up=_col_blocks(w_up[l].astype(BF16), FFN_CHUNK),
        conv_w=conv_w[l], conv_b=conv_b[l][None], w_down=w_down[l].astype(BF16),
        norm_ple=norm_ple[l][None], w_ple_gate=w_ple_gate[l].astype(BF16),
        w_ple_proj=w_ple_proj[l].astype(BF16))


def _trunk(x, p_l, w, hg_lower, norm_final, cs, *, layer, nb, t, final, hg_s0, attend, sample):
    proj = _in_proj(x, w["norm_mix"], w["w_in"])
    ckv, kr, kcat, q = _mla_proj(proj, cs, w["q_norm"], w["kv_norm"], w["wq"], w["wuk"],
                                 F32 if sample else BF16)
    o_hg, hg_s = _hgrn(proj, hg_lower, w["hg_onorm"], hg_s0, layer=layer, nb=nb, t=t,
                       seq_per_step=min(nb, 16) if sample else 1,
                       out_dtype=F32 if sample else BF16)
    o_mla = attend(q, ckv, kr, kcat, w["wuv"])
    x1, h2 = _out_proj(o_hg, o_mla, x, w["wo_hg"], w["wo_mla"], w["norm_ffn"])
    return x1, h2, ckv, kr, hg_s


def kernel(x_prompt, x_sample, cache_ckv, cache_krope, state_hgrn, state_conv, page_table,
           p_prompt, p_sample, norm_mix, w_in, hg_lower, hg_onorm, mla_q_norm, w_q_b,
           mla_kv_norm, w_kv_b, w_out, norm_ffn, w_up, conv_w, conv_b, w_down, norm_ple,
           w_ple_gate, w_ple_proj, norm_final):
    depth = w_in.shape[0]
    bp, tp, d = x_prompt.shape
    bs, ts, _ = x_sample.shape
    dff = w_down.shape[1]
    cs_p = _rope_table(jnp.arange(tp, dtype=jnp.int32))
    cs_s = jnp.tile(_rope_table(PAST_LEN + jnp.arange(ts, dtype=jnp.int32)), (bs, 1))
    nf = norm_final[None]
    xp = x_prompt.reshape(bp * tp, d)
    xs = x_sample.reshape(bs * ts, d)
    outs_p, outs_s = [], []
    for l in range(depth):
        w = _layer_weights(l, norm_mix, w_in, hg_onorm, mla_q_norm, w_q_b, mla_kv_norm, w_kv_b,
                           w_out, norm_ffn, w_up, conv_w, conv_b, w_down, norm_ple, w_ple_gate,
                           w_ple_proj)
        final = l == depth - 1

        attend_p = lambda q, ckv, kr, kcat, wuv: _prompt_attn(q, kcat, wuv, nb=bp, t=tp)
        x1, h2, ckv_p, kr_p, hg_p = _trunk(xp, None, w, hg_lower, nf, cs_p, layer=l, nb=bp, t=tp,
                                           final=final, hg_s0=None, attend=attend_p, sample=False)
        x2, tail = _ffn(h2, x1, w["w_up"], w["w_down"], w["conv_w"], w["conv_b"], None,
                        t=tp, t_sample=0)
        cv_p = tail.reshape(bp, -1, SUBLANES, dff)[:, -1, SUBLANES - (CONV_W - 1):]
        xp = _ple(x2, p_prompt[l].reshape(bp * tp, -1), w["norm_ple"], w["w_ple_gate"],
                  w["w_ple_proj"], nf, final=final)
        outs_p.append((ckv_p.reshape(bp, tp, -1), kr_p.reshape(bp, tp, -1), hg_p, cv_p))

        attend_s = lambda q, ckv, kr, kcat, wuv: _sample_attn(
            q, ckv, kr, wuv, cache_ckv, cache_krope, page_table, layer=l, nb=bs, t=ts)
        x1, h2, ckv_s, kr_s, hg_s = _trunk(xs, None, w, hg_lower, nf, cs_s, layer=l, nb=bs, t=ts,
                                           final=final, hg_s0=state_hgrn[l], attend=attend_s,
                                           sample=True)
        sp = jnp.pad(state_conv[l], ((0, 0), (ts - (CONV_W - 1), 0), (0, 0))).reshape(bs * ts, dff)
        x2, a_full = _ffn(h2, x1, w["w_up"], w["w_down"], w["conv_w"], w["conv_b"], sp,
                          t=ts, t_sample=ts)
        cv_s = a_full.reshape(bs, ts, dff)[:, ts - (CONV_W - 1):]
        xs = _ple(x2, p_sample[l].reshape(bs * ts, -1), w["norm_ple"], w["w_ple_gate"],
                  w["w_ple_proj"], nf, final=final)
        outs_s.append((ckv_s.reshape(bs, ts, -1), kr_s.reshape(bs, ts, -1), hg_s, cv_s))

    stack = lambda outs, k: jnp.stack([o[k] for o in outs])
    return (xp.reshape(bp, tp, d), xs.reshape(bs, ts, d),
            stack(outs_p, 0), stack(outs_p, 1), stack(outs_p, 2), stack(outs_p, 3),
            stack(outs_s, 0), stack(outs_s, 1), stack(outs_s, 2), stack(outs_s, 3))
```

```python
import functools

import jax
import jax.numpy as jnp
from jax import lax
from jax.experimental import pallas as pl
from jax.experimental.pallas import tpu as pltpu

F32 = jnp.float32
BF16 = jnp.bfloat16

HG_HEADS = 8
HG_DK = 128
HG_DV = 128
HG_CHUNK = 64
HG_HEADS_PER_STEP = 1
HG_GROUP_ROWS = 256
IN_PROJ_COLS = 1024
FFN_CHUNK = 512
SAMPLE_PAGES_PER_GROUP = 32
GATHER_AHEAD = 2
GATHER_SLOTS = GATHER_AHEAD + 1
ATTN_SOFTMAX_ROWS = 32
ATTN_SOFTMAX_UNROLL = 32
MLA_HEADS = 8
MLA_Q_LORA = 512
MLA_KV_LORA = 256
MLA_NOPE = 128
MLA_ROPE = 64
MLA_V = 128
MLA_SCALE = (MLA_NOPE + MLA_ROPE) ** -0.5
ROPE_THETA = 10000.0
PAST_LEN = 16384
CONV_W = 3
EPS = 1e-6
NEG_INF = -1e30

HG_COLS = 4 * HG_HEADS * HG_DK
MLA_COLS = 1024
Q_HEAD_COLS = 256
QK_PAD = 384
LANES = 128
SUBLANES = 8

VMEM_LIMIT = 48 * 1024 * 1024

_NT = (((1,), (1,)), ((), ()))
_TN = (((0,), (0,)), ((), ()))


def _cparams(*sem):
    return pltpu.CompilerParams(dimension_semantics=sem, vmem_limit_bytes=VMEM_LIMIT)


def _tile(n, pref):
    t = min(n, pref)
    assert n % t == 0, (n, pref)
    return t


def _rms_rows(x, g):
    ms = jnp.mean(x * x, axis=-1, keepdims=True)
    return (x * lax.rsqrt(ms + EPS)) * g


def _dot(a, b):
    return jnp.dot(a, b, preferred_element_type=F32)


def _dot_nt(a, b):
    return lax.dot_general(a, b, _NT, preferred_element_type=F32)


def _in_proj_kernel(x_ref, g_ref, w_ref, o_ref, h_ref):
    @pl.when(pl.program_id(1) == 0)
    def _():
        h_ref[...] = _rms_rows(x_ref[...], g_ref[...]).astype(BF16)

    o_ref[...] = _dot(h_ref[...], w_ref[...])


def _in_proj(x, g, w):
    n, d = x.shape
    cols = w.shape[1]
    tm = _tile(n, 1024)
    tn = IN_PROJ_COLS
    return pl.pallas_call(
        _in_proj_kernel,
        name="in_proj",
        grid=(n // tm, cols // tn),
        in_specs=[
            pl.BlockSpec((tm, d), lambda i, j: (i, 0)),
            pl.BlockSpec((1, d), lambda i, j: (0, 0)),
            pl.BlockSpec((d, tn), lambda i, j: (0, j)),
        ],
        out_specs=pl.BlockSpec((tm, tn), lambda i, j: (i, j)),
        out_shape=jax.ShapeDtypeStruct((n, cols), F32),
        scratch_shapes=[pltpu.VMEM((tm, d), BF16)],
        compiler_params=_cparams("arbitrary", "arbitrary"),
    )(x, g, w)


def _mla_proj_kernel(p_ref, cs_ref, qn_ref, kvn_ref, wq_ref, wuk_ref,
                     ckv_ref, kr_ref, kcat_ref, q_ref):
    p = p_ref[...]
    cs = cs_ref[...]
    tm = p.shape[0]
    lane = lax.broadcasted_iota(jnp.int32, (tm, LANES), 1)
    cq = _rms_rows(p[:, :MLA_Q_LORA], qn_ref[...]).astype(BF16)
    ckv = _rms_rows(p[:, MLA_Q_LORA:MLA_Q_LORA + MLA_KV_LORA], kvn_ref[...])
    ckv_ref[...] = ckv
    kk = p[:, 768:896] * cs
    kro = kk + pltpu.roll(kk, MLA_ROPE, 1)
    kr_ref[...] = kro[:, :MLA_ROPE]
    kcat_ref[:, :MLA_KV_LORA] = ckv.astype(kcat_ref.dtype)
    kcat_ref[:, MLA_KV_LORA:] = jnp.where(lane < MLA_ROPE, kro, 0.0).astype(kcat_ref.dtype)
    q = _dot(cq, wq_ref[...])
    for h in range(MLA_HEADS):
        qh = q[:, h * Q_HEAD_COLS:(h + 1) * Q_HEAD_COLS]
        qlat = _dot(qh[:, :MLA_NOPE].astype(BF16), wuk_ref[h]) * MLA_SCALE
        rr = qh[:, MLA_NOPE:] * cs
        rope = (rr + pltpu.roll(rr, MLA_ROPE, 1)) * MLA_SCALE
        q_ref[:, h * QK_PAD:h * QK_PAD + MLA_KV_LORA] = qlat.astype(q_ref.dtype)
        q_ref[:, h * QK_PAD + MLA_KV_LORA:(h + 1) * QK_PAD] = (
            jnp.where(lane < MLA_ROPE, rope, 0.0).astype(q_ref.dtype))


def _mla_proj(proj, cs, qn, kvn, wq, wuk, q_dtype):
    n = proj.shape[0]
    tm = _tile(n, 256)
    cs_blocks = cs.shape[0] // tm
    mla_blk = HG_COLS // MLA_COLS
    return pl.pallas_call(
        _mla_proj_kernel,
        name="mla_proj",
        grid=(n // tm,),
        in_specs=[
            pl.BlockSpec((tm, MLA_COLS), lambda i: (i, mla_blk)),
            pl.BlockSpec((tm, LANES), lambda i: (i % cs_blocks, 0)),
            pl.BlockSpec((1, MLA_Q_LORA), lambda i: (0, 0)),
            pl.BlockSpec((1, MLA_KV_LORA), lambda i: (0, 0)),
            pl.BlockSpec(wq.shape, lambda i: (0, 0)),
            pl.BlockSpec(wuk.shape, lambda i: (0, 0, 0)),
        ],
        out_specs=[
            pl.BlockSpec((tm, MLA_KV_LORA), lambda i: (i, 0)),
            pl.BlockSpec((tm, MLA_ROPE), lambda i: (i, 0)),
            pl.BlockSpec((tm, QK_PAD), lambda i: (i, 0)),
            pl.BlockSpec((tm, MLA_HEADS * QK_PAD), lambda i: (i, 0)),
        ],
        out_shape=[
            jax.ShapeDtypeStruct((n, MLA_KV_LORA), F32),
            jax.ShapeDtypeStruct((n, MLA_ROPE), F32),
            jax.ShapeDtypeStruct((n, QK_PAD), BF16),
            jax.ShapeDtypeStruct((n, MLA_HEADS * QK_PAD), q_dtype),
        ],
        compiler_params=_cparams("arbitrary"),
    )(proj, cs, qn, kvn, wq, wuk)


def _hgrn_kernel(*refs, layer, nseq, nchunk, chunk, has_s0):
    if has_s0:
        hl_ref, q_ref, f_ref, i_ref, g_ref, on_ref, s0_ref, o_ref, sout_ref = refs
    else:
        hl_ref, q_ref, f_ref, i_ref, g_ref, on_ref, o_ref, sout_ref = refs
        s0_ref = None
    hl = hl_ref[...]
    e = jnp.exp(hl - jnp.max(hl, axis=0, keepdims=True))
    lb = jnp.sum(e[:layer + 1], axis=0, keepdims=True) / jnp.sum(e, axis=0, keepdims=True)
    onorm = on_ref[...]
    width = hl.shape[1]
    nh = width // HG_DK
    heads = [slice(j * HG_DK, (j + 1) * HG_DK) for j in range(nh)]
    zero_state = jnp.zeros((HG_DV, HG_DK), F32)
    sup = min(HG_GROUP_ROWS // chunk, nseq * nchunk)
    rg = sup * chunk
    ngroups = (nseq * nchunk) // sup
    assert (nseq * nchunk) % sup == 0 and (nseq == 1 or nchunk == 1)
    rix = lax.broadcasted_iota(jnp.int32, (rg, rg), 0)
    cix = lax.broadcasted_iota(jnp.int32, (rg, rg), 1)
    same = (rix // chunk) == (cix // chunk)
    causal = jnp.logical_and(same, rix >= cix)
    scan_lhs = jnp.concatenate([causal, same], axis=0).astype(BF16)

    def group_body(s, states):
        r0 = pl.multiple_of(s * rg, rg)
        qr = q_ref[pl.ds(r0, rg), :]
        fx = f_ref[pl.ds(r0, rg), :]
        v = i_ref[pl.ds(r0, rg), :]
        gr = g_ref[pl.ds(r0, rg), :]
        sg = jax.nn.sigmoid(fx)
        f = lb + (1.0 - lb) * sg
        k = (1.0 - lb) * (1.0 - sg)
        logf = jnp.log(f)
        hi = logf.astype(BF16)
        r1 = logf - hi.astype(F32)
        mid = r1.astype(BF16)
        lo = (r1 - mid.astype(F32)).astype(BF16)
        parts = _dot(scan_lhs, jnp.concatenate([hi, mid, lo], axis=1))
        sums = (parts[:, :width] + parts[:, width:2 * width]) + parts[:, 2 * width:]
        cum = sums[:rg]
        tot = sums[rg:]
        qd = (qr * jax.nn.sigmoid(qr)) * jnp.exp(cum)
        ki = (k * jnp.exp(-cum)).astype(BF16)
        ke = k * jnp.exp(tot - cum)
        dec = jnp.exp(tot)
        gate = jax.nn.sigmoid(gr)
        outs, new_states = [], []
        for j, hs in enumerate(heads):
            st = states[j]
            a = jnp.where(causal, _dot_nt(qd[:, hs].astype(BF16), ki[:, hs]), 0.0)
            o_intra = _dot(a.astype(BF16), v[:, hs].astype(BF16))
            o_state = []
            for c in range(sup):
                cs = slice(c * chunk, (c + 1) * chunk)
                if nchunk == 1:
                    g = s * sup + c
                    st = s0_ref[g, j].T if has_s0 else zero_state
                o_state.append(_dot_nt(qd[cs, hs].astype(BF16), st.astype(BF16)))
                upd = lax.dot_general(v[cs, hs].astype(BF16), ke[cs, hs].astype(BF16), _TN,
                                      preferred_element_type=F32)
                st = st * dec[c * chunk:c * chunk + 1, hs] + upd
                if nchunk == 1:
                    sout_ref[g, j] = st.T
            o = o_intra + jnp.concatenate(o_state, axis=0)
            outs.append(_rms_rows(o, onorm[:, hs]) * gate[:, hs])
            new_states.append(st)
        o_ref[pl.ds(r0, rg), :] = jnp.concatenate(outs, axis=1).astype(o_ref.dtype)
        return tuple(new_states)

    if nchunk == 1:
        lax.fori_loop(0, ngroups, group_body, (zero_state,) * nh)
    else:
        st0 = tuple(s0_ref[0, j].T if has_s0 else zero_state for j in range(nh))
        final = lax.fori_loop(0, ngroups, group_body, st0)
        for j in range(nh):
            sout_ref[0, j] = final[j].T


def _hgrn(proj, hg_lower, onorm, s0, *, layer, nb, t, seq_per_step, out_dtype):
    n = proj.shape[0]
    assert n == nb * t
    chunk = min(t, HG_CHUNK)
    assert t % chunk == 0 and nb % seq_per_step == 0
    rows = seq_per_step * t
    nl = hg_lower.shape[0]
    hp = HG_HEADS_PER_STEP
    hsteps = HG_HEADS // hp
    width = hp * HG_DK

    def col(seg):
        return pl.BlockSpec((rows, width), lambda i, h: (i, seg * hsteps + h))

    in_specs = [
        pl.BlockSpec((nl, width), lambda i, h: (0, h)),
        col(0), col(1), col(2), col(3),
        pl.BlockSpec((1, width), lambda i, h: (0, h)),
    ]
    args = [hg_lower, proj, proj, proj, proj, onorm]
    state_spec = pl.BlockSpec((seq_per_step, hp, HG_DK, HG_DV), lambda i, h: (i, h, 0, 0))
    if s0 is not None:
        in_specs.append(state_spec)
        args.append(s0)
    kern = functools.partial(_hgrn_kernel, layer=layer, nseq=seq_per_step, nchunk=t // chunk,
                             chunk=chunk, has_s0=s0 is not None)
    return pl.pallas_call(
        kern,
        name="hgrn",
        grid=(nb // seq_per_step, hsteps),
        in_specs=in_specs,
        out_specs=[
            pl.BlockSpec((rows, width), lambda i, h: (i, h)),
            state_spec,
        ],
        out_shape=[
            jax.ShapeDtypeStruct((n, HG_HEADS * HG_DV), out_dtype),
            jax.ShapeDtypeStruct((nb, HG_HEADS, HG_DK, HG_DV), F32),
        ],
        compiler_params=_cparams("arbitrary", "arbitrary"),
    )(*args)


def _prompt_attn_kernel(q_ref, k_ref, wuv_ref, o_ref, qs_ref, s_ref, p_ref, m_ref, l_ref, acc_ref,
                        *, tq, tk):
    qi = pl.program_id(1)
    rows = MLA_HEADS * tq
    nlg = tk // LANES
    rb = ATTN_SOFTMAX_ROWS
    for h in range(MLA_HEADS):
        qs_ref[h * tq:(h + 1) * tq, :] = q_ref[:, h * QK_PAD:(h + 1) * QK_PAD]
    m_ref[...] = jnp.full((rows, LANES), NEG_INF, F32)
    l_ref[...] = jnp.zeros((rows, LANES), F32)
    acc_ref[...] = jnp.zeros((rows, MLA_KV_LORA), F32)
    nfull = (qi * tq) // tk

    def softmax_rows(r, limit):
        rs = pl.ds(pl.multiple_of(r * rb, rb), rb)
        sb = [s_ref[rs, i * LANES:(i + 1) * LANES] for i in range(nlg)]
        if limit is not None:
            tok = lax.rem(r * rb + lax.broadcasted_iota(jnp.int32, (rb, LANES), 0), tq)
            lane = lax.broadcasted_iota(jnp.int32, (rb, LANES), 1)
            sb = [jnp.where(lane + i * LANES - tok <= limit, sb[i], NEG_INF) for i in range(nlg)]
        mx = functools.reduce(jnp.maximum, sb)
        m_prev = m_ref[rs, :]
        m_new = jnp.maximum(m_prev, jnp.max(mx, axis=-1, keepdims=True))
        corr = jnp.exp(m_prev - m_new)
        ps = [jnp.exp(x - m_new) for x in sb]
        l_ref[rs, :] = l_ref[rs, :] * corr + jnp.sum(functools.reduce(jnp.add, ps), axis=-1,
                                                     keepdims=True)
        m_ref[rs, :] = m_new
        for i in range(nlg):
            p_ref[rs, i * LANES:(i + 1) * LANES] = ps[i].astype(BF16)
        for i in range(MLA_KV_LORA // LANES):
            acc_ref[rs, i * LANES:(i + 1) * LANES] = acc_ref[rs, i * LANES:(i + 1) * LANES] * corr

    def block(j, masked):
        kb = k_ref[pl.ds(pl.multiple_of(j * tk, tk), tk), :]
        hk = tk // 2
        s_ref[:, :hk] = _dot_nt(qs_ref[...], kb[:hk])
        s_ref[:, hk:] = _dot_nt(qs_ref[...], kb[hk:])
        limit = qi * tq - j * tk if masked else None

        def sweep(r, carry):
            softmax_rows(r, limit)
            return carry

        lax.fori_loop(0, rows // rb, sweep, 0, unroll=ATTN_SOFTMAX_UNROLL)
        hr = rows // 2
        acc_ref[:hr, :] += _dot(p_ref[:hr, :], kb[:, :MLA_KV_LORA])
        acc_ref[hr:, :] += _dot(p_ref[hr:, :], kb[:, :MLA_KV_LORA])

    def body(j, carry):
        block(j, False)
        return carry

    lax.fori_loop(0, nfull, body, 0)
    block(nfull, True)
    for h in range(MLA_HEADS):
        hs = slice(h * tq, (h + 1) * tq)
        inv = 1.0 / l_ref[hs, :]
        lat = jnp.concatenate([acc_ref[hs, i * LANES:(i + 1) * LANES] * inv
                               for i in range(MLA_KV_LORA // LANES)], axis=1)
        o_ref[:, h * MLA_V:(h + 1) * MLA_V] = _dot(lat.astype(BF16), wuv_ref[h]).astype(o_ref.dtype)


def _prompt_attn(q, kcat, wuv, *, nb, t):
    n = q.shape[0]
    tq = _tile(t, 128)
    tk = _tile(t, 512)
    assert tk % tq == 0
    nq = t // tq
    rows = MLA_HEADS * tq
    return pl.pallas_call(
        functools.partial(_prompt_attn_kernel, tq=tq, tk=tk),
        name="prompt_attn",
        grid=(nb, nq),
        in_specs=[
            pl.BlockSpec((tq, MLA_HEADS * QK_PAD), lambda b, i: (b * nq + i, 0)),
            pl.BlockSpec((t, QK_PAD), lambda b, i: (b, 0)),
            pl.BlockSpec(wuv.shape, lambda b, i: (0, 0, 0)),
        ],
        out_specs=pl.BlockSpec((tq, MLA_HEADS * MLA_V), lambda b, i: (b * nq + i, 0)),
        out_shape=jax.ShapeDtypeStruct((n, MLA_HEADS * MLA_V), BF16),
        scratch_shapes=[
            pltpu.VMEM((rows, QK_PAD), BF16),
            pltpu.VMEM((rows, tk), F32),
            pltpu.VMEM((rows, tk), BF16),
            pltpu.VMEM((rows, LANES), F32),
            pltpu.VMEM((rows, LANES), F32),
            pltpu.VMEM((rows, MLA_KV_LORA), F32),
        ],
        compiler_params=_cparams("arbitrary", "arbitrary"),
    )(q, kcat, wuv)


def _sample_attn_kernel(pt_ref, q_ref, ckvn_ref, krn_ref, wuv_ref, ckv_hbm, krt_hbm, o_ref,
                        kbuf, rbuf, sem, kc_ref, p_ref, qs_ref, knc_ref, knr_ref,
                        *, layer, n_pages, npg, nb, t):
    b = pl.program_id(0)
    ngroups = n_pages // npg
    rows = MLA_HEADS * t
    page = knc_ref.shape[0]

    def group_copies(n):
        slot = lax.rem(n, GATHER_SLOTS)
        copies = []
        for k in range(npg):
            pid = pt_ref[n * npg + k]
            copies.append(pltpu.make_async_copy(
                ckv_hbm.at[layer, pid], kbuf.at[slot, pl.ds(k * page, page), :], sem.at[slot]))
            copies.append(pltpu.make_async_copy(
                krt_hbm.at[layer, pid], rbuf.at[slot, k], sem.at[slot]))
        return copies

    @pl.when(b == 0)
    def _():
        for n in range(GATHER_AHEAD):
            for c in group_copies(n):
                c.start()

    for h in range(MLA_HEADS):
        qs_ref[h * t:(h + 1) * t, :] = q_ref[:, h * QK_PAD:(h + 1) * QK_PAD]
    qb = qs_ref[...].astype(BF16)
    ql = qb[:, :MLA_KV_LORA]
    qr = qb[:, MLA_KV_LORA:MLA_KV_LORA + MLA_ROPE]

    def softmax_step(s, m, l):
        m_new = jnp.maximum(m, jnp.max(s, axis=-1, keepdims=True))
        corr = jnp.exp(m - m_new)
        p = jnp.exp(s - m_new)
        return p.astype(BF16), corr, m_new, l * corr + jnp.sum(p, axis=-1, keepdims=True)

    m = jnp.full((rows, 1), NEG_INF, F32)
    l = jnp.zeros((rows, 1), F32)
    acc = jnp.zeros((rows, MLA_KV_LORA), F32)
    pending = None
    for g in range(ngroups):
        n = b * ngroups + g
        gslot = lax.rem(n, GATHER_SLOTS)
        slot = g % 2 if ngroups % 2 == 0 else lax.rem(n, 2)
        for c in group_copies(n):
            c.wait()
        for c in group_copies(n + GATHER_AHEAD):
            c.start()
        kc_ref[slot] = kbuf[gslot].astype(BF16)
        s_rope = jnp.concatenate(
            [_dot(qr, rbuf[gslot, k].astype(BF16)) for k in range(npg)], axis=1)
        s = _dot_nt(ql, kc_ref[slot]) + s_rope
        p, corr, m, l = softmax_step(s, m, l)
        p_ref[slot] = p
        if pending is not None:
            pslot, pcorr = pending
            acc = acc * pcorr + _dot(p_ref[pslot], kc_ref[pslot])
        pending = (slot, corr)
    pslot, pcorr = pending
    acc = acc * pcorr + _dot(p_ref[pslot], kc_ref[pslot])

    @pl.when(b == nb - 1)
    def _():
        for n in range(GATHER_AHEAD):
            for c in group_copies(nb * ngroups + n):
                c.wait()

    knc_ref[...] = jnp.zeros(knc_ref.shape, F32)
    knr_ref[...] = jnp.zeros(knr_ref.shape, F32)
    knc_ref[:t, :] = ckvn_ref[...]
    knr_ref[:t, :] = krn_ref[...]
    kc = knc_ref[...].astype(BF16)
    s = _dot_nt(ql, kc) + _dot_nt(qr, knr_ref[...].astype(BF16))
    tq = lax.rem(lax.broadcasted_iota(jnp.int32, (rows, page), 0), t)
    kidx = lax.broadcasted_iota(jnp.int32, (rows, page), 1)
    p, corr, m, l = softmax_step(jnp.where(kidx <= tq, s, NEG_INF), m, l)
    acc = acc * corr + _dot(p, kc)
    lat = acc / l
    for h in range(MLA_HEADS):
        o_ref[:, h * MLA_V:(h + 1) * MLA_V] = _dot(
            lat[h * t:(h + 1) * t].astype(BF16), wuv_ref[h]).astype(o_ref.dtype)


def _sample_attn(q, ckv_new, kr_new, wuv, cache_ckv, cache_krope, page_table, *, layer, nb, t):
    n_pages = page_table.shape[1]
    page = cache_ckv.shape[2]
    npg = min(n_pages, SAMPLE_PAGES_PER_GROUP)
    assert n_pages % npg == 0 and t <= page
    rows = MLA_HEADS * t
    pt_flat = page_table.reshape(-1)
    pt_flat = jnp.concatenate([pt_flat, pt_flat[:GATHER_AHEAD * npg]])
    krope_t = jnp.swapaxes(cache_krope, 2, 3)
    in_specs = [
        pl.BlockSpec((t, MLA_HEADS * QK_PAD), lambda b, pt: (b, 0)),
        pl.BlockSpec((t, MLA_KV_LORA), lambda b, pt: (b, 0)),
        pl.BlockSpec((t, MLA_ROPE), lambda b, pt: (b, 0)),
        pl.BlockSpec(wuv.shape, lambda b, pt: (0, 0, 0)),
        pl.BlockSpec(memory_space=pl.ANY),
        pl.BlockSpec(memory_space=pl.ANY),
    ]
    grid_spec = pltpu.PrefetchScalarGridSpec(
        num_scalar_prefetch=1,
        grid=(nb,),
        in_specs=in_specs,
        out_specs=pl.BlockSpec((t, MLA_HEADS * MLA_V), lambda b, pt: (b, 0)),
        scratch_shapes=[
            pltpu.VMEM((GATHER_SLOTS, npg * page, MLA_KV_LORA), F32),
            pltpu.VMEM((GATHER_SLOTS, npg, MLA_ROPE, page), F32),
            pltpu.SemaphoreType.DMA((GATHER_SLOTS,)),
            pltpu.VMEM((2, npg * page, MLA_KV_LORA), BF16),
            pltpu.VMEM((2, rows, npg * page), BF16),
            pltpu.VMEM((rows, QK_PAD), F32),
            pltpu.VMEM((page, MLA_KV_LORA), F32),
            pltpu.VMEM((page, MLA_ROPE), F32),
        ],
    )
    return pl.pallas_call(
        functools.partial(_sample_attn_kernel, layer=layer, n_pages=n_pages, npg=npg, nb=nb, t=t),
        name="sample_attn",
        grid_spec=grid_spec,
        out_shape=jax.ShapeDtypeStruct((nb * t, MLA_HEADS * MLA_V), F32),
        compiler_params=_cparams("arbitrary"),
    )(pt_flat, q, ckv_new, kr_new, wuv, cache_ckv, krope_t)


def _out_proj_kernel(ohg_ref, omla_ref, x_ref, w1_ref, w2_ref, g_ref, x1_ref, h2_ref):
    y = _dot(ohg_ref[...].astype(BF16), w1_ref[...]) + _dot(omla_ref[...].astype(BF16), w2_ref[...])
    x1 = x_ref[...] + y
    x1_ref[...] = x1
    h2_ref[...] = _rms_rows(x1, g_ref[...]).astype(BF16)


def _out_proj(ohg, omla, x, w1, w2, g):
    n, d = x.shape
    tm = _tile(n, 512)
    row = lambda i: (i, 0)
    fixed = lambda i: (0, 0)
    return pl.pallas_call(
        _out_proj_kernel,
        name="out_proj",
        grid=(n // tm,),
        in_specs=[
            pl.BlockSpec((tm, ohg.shape[1]), row),
            pl.BlockSpec((tm, omla.shape[1]), row),
            pl.BlockSpec((tm, d), row),
            pl.BlockSpec(w1.shape, fixed),
            pl.BlockSpec(w2.shape, fixed),
            pl.BlockSpec((1, d), fixed),
        ],
        out_specs=[pl.BlockSpec((tm, d), row), pl.BlockSpec((tm, d), row)],
        out_shape=[jax.ShapeDtypeStruct((n, d), F32), jax.ShapeDtypeStruct((n, d), BF16)],
        compiler_params=_cparams("arbitrary"),
    )(ohg, omla, x, w1, w2, g)


def _ffn_kernel(*refs, tm, nc, tiles_per_seq, t_sample):
    if t_sample:
        (h_ref, x_ref, wa_ref, wg_ref, wd_ref, cw_ref, cb_ref, sp_ref, o_ref, tail_ref,
         act_ref) = refs
    else:
        (h_ref, x_ref, wa_ref, wg_ref, wd_ref, cw_ref, cb_ref, o_ref, tail_ref,
         act_ref, halo_ref) = refs
    i = pl.program_id(0)
    c = pl.program_id(1)
    tc = wa_ref.shape[1]

    def up_and_activate():
        h = h_ref[...]
        a = _dot(h, wa_ref[...])
        gate = _dot(h, wg_ref[...])
        r1 = pltpu.roll(a, 1, 0)
        r2 = pltpu.roll(a, 2, 0)
        if t_sample:
            sp = sp_ref[...]
            tt = lax.rem(lax.broadcasted_iota(jnp.int32, (tm, tc), 0), t_sample)
            a1 = jnp.where(tt == 0, pltpu.roll(sp, tm - (t_sample - 1), 0), r1)
            a2 = jnp.where(tt < 2, pltpu.roll(sp, tm - (t_sample - 2), 0), r2)
            tail_ref[...] = a
        else:
            halo = jnp.where(lax.rem(i, tiles_per_seq) == 0, 0.0, halo_ref[c])
            row8 = lax.broadcasted_iota(jnp.int32, (SUBLANES, tc), 0)
            top1 = jnp.where(row8 < 1, pltpu.roll(halo, 1, 0), r1[:SUBLANES])
            top2 = jnp.where(row8 < 2, pltpu.roll(halo, 2, 0), r2[:SUBLANES])
            a1 = jnp.concatenate([top1, r1[SUBLANES:]], axis=0)
            a2 = jnp.concatenate([top2, r2[SUBLANES:]], axis=0)
            halo_ref[c] = a[tm - SUBLANES:]
            tail_ref[0] = a[tm - SUBLANES:]
        cw = cw_ref[...]
        conv = ((cb_ref[...] + cw[0:1] * a2) + cw[1:2] * a1) + cw[2:3] * a
        act_ref[lax.rem(c, 2)] = ((conv * jax.nn.sigmoid(conv)) * gate).astype(BF16)

    def down_previous():
        o_ref[...] += _dot(act_ref[lax.rem(c + 1, 2)], wd_ref[...])

    @pl.when(c == 0)
    def _():
        if not t_sample:
            @pl.when(i == 0)
            def _():
                halo_ref[...] = jnp.zeros(halo_ref.shape, F32)
        o_ref[...] = x_ref[...]
        up_and_activate()

    @pl.when(jnp.logical_and(c > 0, c < nc))
    def _():
        up_and_activate()
        down_previous()

    @pl.when(c == nc)
    def _():
        down_previous()


def _ffn(h2, x1, w_up, w_down, conv_w, conv_b, sp, *, t, t_sample):
    n, d = x1.shape
    dff = w_down.shape[0]
    tc = FFN_CHUNK
    nc = dff // tc
    assert dff % tc == 0
    up = lambda c: jnp.minimum(c, nc - 1)
    if t_sample:
        tm = _tile(n, 512)
        assert tm % t_sample == 0
        tiles_per_seq = 1
        tail_shape = (n, dff)
        tail_spec = pl.BlockSpec((tm, tc), lambda i, c: (i, up(c)))
    else:
        tm = _tile(t, 512)
        tiles_per_seq = t // tm
        tail_shape = (n // tm, SUBLANES, dff)
        tail_spec = pl.BlockSpec((1, SUBLANES, tc), lambda i, c: (i, 0, up(c)))
    in_specs = [
        pl.BlockSpec((tm, d), lambda i, c: (i, 0)),
        pl.BlockSpec((tm, d), lambda i, c: (i, 0)),
        pl.BlockSpec((d, tc), lambda i, c: (0, up(c))),
        pl.BlockSpec((d, tc), lambda i, c: (0, nc + up(c))),
        pl.BlockSpec((tc, d), lambda i, c: (jnp.maximum(c - 1, 0), 0)),
        pl.BlockSpec((CONV_W, tc), lambda i, c: (0, up(c))),
        pl.BlockSpec((1, tc), lambda i, c: (0, up(c))),
    ]
    args = [h2, x1, w_up, w_up, w_down, conv_w, conv_b]
    scratch = [pltpu.VMEM((2, tm, tc), BF16)]
    if t_sample:
        in_specs.append(pl.BlockSpec((tm, tc), lambda i, c: (i, up(c))))
        args.append(sp)
    else:
        scratch.append(pltpu.VMEM((nc, SUBLANES, tc), F32))
    return pl.pallas_call(
        functools.partial(_ffn_kernel, tm=tm, nc=nc, tiles_per_seq=tiles_per_seq,
                          t_sample=t_sample),
        name="ffn",
        grid=(n // tm, nc + 1),
        in_specs=in_specs,
        out_specs=[pl.BlockSpec((tm, d), lambda i, c: (i, 0)), tail_spec],
        out_shape=[jax.ShapeDtypeStruct((n, d), F32), jax.ShapeDtypeStruct(tail_shape, F32)],
        scratch_shapes=scratch,
        compiler_params=_cparams("arbitrary", "arbitrary"),
    )(*args)


def _ple_kernel(x_ref, p_ref, gn_ref, wg_ref, wp_ref, fn_ref, y_ref, *, final):
    x = x_ref[...]
    hn = _rms_rows(x, gn_ref[...]).astype(BF16)
    gate = jax.nn.sigmoid(_dot(hn, wg_ref[...]))
    x3 = x + gate * _dot(p_ref[...].astype(BF16), wp_ref[...])
    y_ref[...] = _rms_rows(x3, fn_ref[...]) if final else x3


def _ple(x2, p, gn, wg, wp, fn, *, final):
    n, d = x2.shape
    tm = _tile(n, 512)
    row = lambda i: (i, 0)
    fixed = lambda i: (0, 0)
    return pl.pallas_call(
        functools.partial(_ple_kernel, final=final),
        name="ple",
        grid=(n // tm,),
        in_specs=[
            pl.BlockSpec((tm, d), row),
            pl.BlockSpec((tm, p.shape[1]), row),
            pl.BlockSpec((1, d), fixed),
            pl.BlockSpec(wg.shape, fixed),
            pl.BlockSpec(wp.shape, fixed),
            pl.BlockSpec((1, d), fixed),
        ],
        out_specs=pl.BlockSpec((tm, d), row),
        out_shape=jax.ShapeDtypeStruct((n, d), F32),
        compiler_params=_cparams("arbitrary"),
    )(x2, p, gn, wg, wp, fn)


def _rot_cols(w):
    half = MLA_ROPE // 2
    return jnp.concatenate([-w[..., half:], w[..., :half]], axis=-1)


def _rope_table(pos):
    half = MLA_ROPE // 2
    inv = 1.0 / (ROPE_THETA ** (jnp.arange(half, dtype=F32) / half))
    ang = pos.astype(F32)[:, None] * inv[None, :]
    c, s = jnp.cos(ang), jnp.sin(ang)
    return jnp.concatenate([c, c, s, s], axis=-1)


def _layer_weights(l, norm_mix, w_in, hg_onorm, mla_q_norm, w_q_b, mla_kv_norm, w_kv_b, w_out,
                   norm_ffn, w_up, conv_w, conv_b, w_down, norm_ple, w_ple_gate, w_ple_proj):
    d = w_in.shape[1]
    wi = w_in[l]
    kr_cols = wi[:, HG_COLS + MLA_Q_LORA + MLA_KV_LORA:]
    pad = MLA_COLS - (MLA_Q_LORA + MLA_KV_LORA + 2 * MLA_ROPE)
    w_in_ext = jnp.concatenate([wi, _rot_cols(kr_cols), jnp.zeros((d, pad), wi.dtype)],
                               axis=1).astype(BF16)
    wq = w_q_b[l].reshape(MLA_Q_LORA, MLA_HEADS, MLA_NOPE + MLA_ROPE)
    wq_ext = jnp.concatenate([wq, _rot_cols(wq[..., MLA_NOPE:])], axis=-1)
    wq_ext = wq_ext.reshape(MLA_Q_LORA, MLA_HEADS * Q_HEAD_COLS).astype(BF16)
    wkv = w_kv_b[l].reshape(MLA_KV_LORA, MLA_HEADS, MLA_NOPE + MLA_V)
    wuk = jnp.transpose(wkv[..., :MLA_NOPE], (1, 2, 0)).astype(BF16)
    wuv = jnp.transpose(wkv[..., MLA_NOPE:], (1, 0, 2)).astype(BF16)
    wo = w_out[l].astype(BF16)
    hgw = HG_HEADS * HG_DV
    return dict(
        norm_mix=norm_mix[l][None], w_in=w_in_ext, hg_onorm=hg_onorm[l][None],
        q_norm=mla_q_norm[l][None], wq=wq_ext, kv_norm=mla_kv_norm[l][None], wuk=wuk, wuv=wuv,
        wo_hg=wo[:hgw], wo_mla=wo[hgw:], norm_ffn=norm_ffn[l][None], w_up=w_up[l].astype(BF16),
        conv_w=conv_w[l], conv_b=conv_b[l][None], w_down=w_down[l].astype(BF16),
        norm_ple=norm_ple[l][None], w_ple_gate=w_ple_gate[l].astype(BF16),
        w_ple_proj=w_ple_proj[l].astype(BF16))


def _trunk(x, p_l, w, hg_lower, norm_final, cs, *, layer, nb, t, final, hg_s0, attend, sample):
    proj = _in_proj(x, w["norm_mix"], w["w_in"])
    ckv, kr, kcat, q = _mla_proj(proj, cs, w["q_norm"], w["kv_norm"], w["wq"], w["wuk"],
                                 F32 if sample else BF16)
    o_hg, hg_s = _hgrn(proj, hg_lower, w["hg_onorm"], hg_s0, layer=layer, nb=nb, t=t,
                       seq_per_step=min(nb, 16) if sample else 1,
                       out_dtype=F32 if sample else BF16)
    o_mla = attend(q, ckv, kr, kcat, w["wuv"])
    x1, h2 = _out_proj(o_hg, o_mla, x, w["wo_hg"], w["wo_mla"], w["norm_ffn"])
    return x1, h2, ckv, kr, hg_s


def kernel(x_prompt, x_sample, cache_ckv, cache_krope, state_hgrn, state_conv, page_table,
           p_prompt, p_sample, norm_mix, w_in, hg_lower, hg_onorm, mla_q_norm, w_q_b,
           mla_kv_norm, w_kv_b, w_out, norm_ffn, w_up, conv_w, conv_b, w_down, norm_ple,
           w_ple_gate, w_ple_proj, norm_final):
    depth = w_in.shape[0]
    bp, tp, d = x_prompt.shape
    bs, ts, _ = x_sample.shape
    dff = w_down.shape[1]
    cs_p = _rope_table(jnp.arange(tp, dtype=jnp.int32))
    cs_s = jnp.tile(_rope_table(PAST_LEN + jnp.arange(ts, dtype=jnp.int32)), (bs, 1))
    nf = norm_final[None]
    xp = x_prompt.reshape(bp * tp, d)
    xs = x_sample.reshape(bs * ts, d)
    outs_p, outs_s = [], []
    for l in range(depth):
        w = _layer_weights(l, norm_mix, w_in, hg_onorm, mla_q_norm, w_q_b, mla_kv_norm, w_kv_b,
                           w_out, norm_ffn, w_up, conv_w, conv_b, w_down, norm_ple, w_ple_gate,
                           w_ple_proj)
        final = l == depth - 1

        attend_p = lambda q, ckv, kr, kcat, wuv: _prompt_attn(q, kcat, wuv, nb=bp, t=tp)
        x1, h2, ckv_p, kr_p, hg_p = _trunk(xp, None, w, hg_lower, nf, cs_p, layer=l, nb=bp, t=tp,
                                           final=final, hg_s0=None, attend=attend_p, sample=False)
        x2, tail = _ffn(h2, x1, w["w_up"], w["w_down"], w["conv_w"], w["conv_b"], None,
                        t=tp, t_sample=0)
        cv_p = tail.reshape(bp, -1, SUBLANES, dff)[:, -1, SUBLANES - (CONV_W - 1):]
        xp = _ple(x2, p_prompt[l].reshape(bp * tp, -1), w["norm_ple"], w["w_ple_gate"],
                  w["w_ple_proj"], nf, final=final)
        outs_p.append((ckv_p.reshape(bp, tp, -1), kr_p.reshape(bp, tp, -1), hg_p, cv_p))

        attend_s = lambda q, ckv, kr, kcat, wuv: _sample_attn(
            q, ckv, kr, wuv, cache_ckv, cache_krope, page_table, layer=l, nb=bs, t=ts)
        x1, h2, ckv_s, kr_s, hg_s = _trunk(xs, None, w, hg_lower, nf, cs_s, layer=l, nb=bs, t=ts,
                                           final=final, hg_s0=state_hgrn[l], attend=attend_s,
                                           sample=True)
        sp = jnp.pad(state_conv[l], ((0, 0), (ts - (CONV_W - 1), 0), (0, 0))).reshape(bs * ts, dff)
        x2, a_full = _ffn(h2, x1, w["w_up"], w["w_down"], w["conv_w"], w["conv_b"], sp,
                          t=ts, t_sample=ts)
        cv_s = a_full.reshape(bs, ts, dff)[:, ts - (CONV_W - 1):]
        xs = _ple(x2, p_sample[l].reshape(bs * ts, -1), w["norm_ple"], w["w_ple_gate"],
                  w["w_ple_proj"], nf, final=final)
        outs_s.append((ckv_s.reshape(bs, ts, -1), kr_s.reshape(bs, ts, -1), hg_s, cv_s))

    stack = lambda outs, k: jnp.stack([o[k] for o in outs])
    return (xp.reshape(bp, tp, d), xs.reshape(bs, ts, d),
            stack(outs_p, 0), stack(outs_p, 1), stack(outs_p, 2), stack(outs_p, 3),
            stack(outs_s, 0), stack(outs_s, 1), stack(outs_s, 2), stack(outs_s, 3))
```

```python
import functools

import jax
import jax.numpy as jnp
from jax import lax
from jax.experimental import pallas as pl
from jax.experimental.pallas import tpu as pltpu

F32 = jnp.float32
BF16 = jnp.bfloat16

HG_HEADS = 8
HG_DK = 128
HG_DV = 128
HG_CHUNK = 64
HG_HEADS_PER_STEP = 1
HG_GROUP_ROWS = 256
IN_PROJ_COLS = 1024
FFN_CHUNK = 512
SAMPLE_PAGES_PER_GROUP = 32
GATHER_AHEAD = 2
GATHER_SLOTS = GATHER_AHEAD + 1
ATTN_SOFTMAX_ROWS = 32
ATTN_SOFTMAX_UNROLL = 32
MLA_HEADS = 8
MLA_Q_LORA = 512
MLA_KV_LORA = 256
MLA_NOPE = 128
MLA_ROPE = 64
MLA_V = 128
MLA_SCALE = (MLA_NOPE + MLA_ROPE) ** -0.5
ROPE_THETA = 10000.0
PAST_LEN = 16384
CONV_W = 3
EPS = 1e-6
NEG_INF = -1e30

HG_COLS = 4 * HG_HEADS * HG_DK
MLA_COLS = 1024
Q_HEAD_COLS = 256
QK_PAD = 384
LANES = 128
SUBLANES = 8

VMEM_LIMIT = 48 * 1024 * 1024

_NT = (((1,), (1,)), ((), ()))
_TN = (((0,), (0,)), ((), ()))


def _cparams(*sem):
    return pltpu.CompilerParams(dimension_semantics=sem, vmem_limit_bytes=VMEM_LIMIT)


def _tile(n, pref):
    t = min(n, pref)
    assert n % t == 0, (n, pref)
    return t


def _rms_rows(x, g):
    ms = jnp.mean(x * x, axis=-1, keepdims=True)
    return (x * lax.rsqrt(ms + EPS)) * g


def _dot(a, b):
    return jnp.dot(a, b, preferred_element_type=F32)


def _dot_nt(a, b):
    return lax.dot_general(a, b, _NT, preferred_element_type=F32)


def _in_proj_kernel(x_ref, g_ref, w_ref, o_ref, h_ref):
    @pl.when(pl.program_id(1) == 0)
    def _():
        h_ref[...] = _rms_rows(x_ref[...], g_ref[...]).astype(BF16)

    o_ref[...] = _dot_nt(h_ref[...], w_ref[...])


def _in_proj(x, g, w):
    n, d = x.shape
    cols = w.shape[0]
    tm = _tile(n, 1024)
    tn = IN_PROJ_COLS
    return pl.pallas_call(
        _in_proj_kernel,
        name="in_proj",
        grid=(n // tm, cols // tn),
        in_specs=[
            pl.BlockSpec((tm, d), lambda i, j: (i, 0)),
            pl.BlockSpec((1, d), lambda i, j: (0, 0)),
            pl.BlockSpec((tn, d), lambda i, j: (j, 0)),
        ],
        out_specs=pl.BlockSpec((tm, tn), lambda i, j: (i, j)),
        out_shape=jax.ShapeDtypeStruct((n, cols), F32),
        scratch_shapes=[pltpu.VMEM((tm, d), BF16)],
        compiler_params=_cparams("arbitrary", "arbitrary"),
    )(x, g, w)


def _mla_proj_kernel(p_ref, cs_ref, qn_ref, kvn_ref, wq_ref, wuk_ref,
                     ckv_ref, kr_ref, kcat_ref, q_ref):
    p = p_ref[...]
    cs = cs_ref[...]
    tm = p.shape[0]
    lane = lax.broadcasted_iota(jnp.int32, (tm, LANES), 1)
    cq = _rms_rows(p[:, :MLA_Q_LORA], qn_ref[...]).astype(BF16)
    ckv = _rms_rows(p[:, MLA_Q_LORA:MLA_Q_LORA + MLA_KV_LORA], kvn_ref[...])
    ckv_ref[...] = ckv
    kk = p[:, 768:896] * cs
    kro = kk + pltpu.roll(kk, MLA_ROPE, 1)
    kr_ref[...] = kro[:, :MLA_ROPE]
    kcat_ref[:, :MLA_KV_LORA] = ckv.astype(kcat_ref.dtype)
    kcat_ref[:, MLA_KV_LORA:] = jnp.where(lane < MLA_ROPE, kro, 0.0).astype(kcat_ref.dtype)
    q = _dot(cq, wq_ref[...])
    for h in range(MLA_HEADS):
        qh = q[:, h * Q_HEAD_COLS:(h + 1) * Q_HEAD_COLS]
        qlat = _dot(qh[:, :MLA_NOPE].astype(BF16), wuk_ref[h]) * MLA_SCALE
        rr = qh[:, MLA_NOPE:] * cs
        rope = (rr + pltpu.roll(rr, MLA_ROPE, 1)) * MLA_SCALE
        q_ref[:, h * QK_PAD:h * QK_PAD + MLA_KV_LORA] = qlat.astype(q_ref.dtype)
        q_ref[:, h * QK_PAD + MLA_KV_LORA:(h + 1) * QK_PAD] = (
            jnp.where(lane < MLA_ROPE, rope, 0.0).astype(q_ref.dtype))


def _mla_proj(proj, cs, qn, kvn, wq, wuk, q_dtype):
    n = proj.shape[0]
    tm = _tile(n, 256)
    cs_blocks = cs.shape[0] // tm
    mla_blk = HG_COLS // MLA_COLS
    return pl.pallas_call(
        _mla_proj_kernel,
        name="mla_proj",
        grid=(n // tm,),
        in_specs=[
            pl.BlockSpec((tm, MLA_COLS), lambda i: (i, mla_blk)),
            pl.BlockSpec((tm, LANES), lambda i: (i % cs_blocks, 0)),
            pl.BlockSpec((1, MLA_Q_LORA), lambda i: (0, 0)),
            pl.BlockSpec((1, MLA_KV_LORA), lambda i: (0, 0)),
            pl.BlockSpec(wq.shape, lambda i: (0, 0)),
            pl.BlockSpec(wuk.shape, lambda i: (0, 0, 0)),
        ],
        out_specs=[
            pl.BlockSpec((tm, MLA_KV_LORA), lambda i: (i, 0)),
            pl.BlockSpec((tm, MLA_ROPE), lambda i: (i, 0)),
            pl.BlockSpec((tm, QK_PAD), lambda i: (i, 0)),
            pl.BlockSpec((tm, MLA_HEADS * QK_PAD), lambda i: (i, 0)),
        ],
        out_shape=[
            jax.ShapeDtypeStruct((n, MLA_KV_LORA), F32),
            jax.ShapeDtypeStruct((n, MLA_ROPE), F32),
            jax.ShapeDtypeStruct((n, QK_PAD), BF16),
            jax.ShapeDtypeStruct((n, MLA_HEADS * QK_PAD), q_dtype),
        ],
        compiler_params=_cparams("arbitrary"),
    )(proj, cs, qn, kvn, wq, wuk)


def _hgrn_kernel(*refs, layer, nseq, nchunk, chunk, has_s0):
    if has_s0:
        hl_ref, q_ref, f_ref, i_ref, g_ref, on_ref, s0_ref, o_ref, sout_ref = refs
    else:
        hl_ref, q_ref, f_ref, i_ref, g_ref, on_ref, o_ref, sout_ref = refs
        s0_ref = None
    hl = hl_ref[...]
    e = jnp.exp(hl - jnp.max(hl, axis=0, keepdims=True))
    lb = jnp.sum(e[:layer + 1], axis=0, keepdims=True) / jnp.sum(e, axis=0, keepdims=True)
    onorm = on_ref[...]
    width = hl.shape[1]
    nh = width // HG_DK
    heads = [slice(j * HG_DK, (j + 1) * HG_DK) for j in range(nh)]
    zero_state = jnp.zeros((HG_DV, HG_DK), F32)
    sup = min(HG_GROUP_ROWS // chunk, nseq * nchunk)
    rg = sup * chunk
    ngroups = (nseq * nchunk) // sup
    assert (nseq * nchunk) % sup == 0 and (nseq == 1 or nchunk == 1)
    rix = lax.broadcasted_iota(jnp.int32, (rg, rg), 0)
    cix = lax.broadcasted_iota(jnp.int32, (rg, rg), 1)
    same = (rix // chunk) == (cix // chunk)
    causal = jnp.logical_and(same, rix >= cix)
    scan_lhs = jnp.concatenate([causal, same], axis=0).astype(BF16)

    def group_body(s, states):
        r0 = pl.multiple_of(s * rg, rg)
        qr = q_ref[pl.ds(r0, rg), :]
        fx = f_ref[pl.ds(r0, rg), :]
        v = i_ref[pl.ds(r0, rg), :]
        gr = g_ref[pl.ds(r0, rg), :]
        sg = jax.nn.sigmoid(fx)
        f = lb + (1.0 - lb) * sg
        k = (1.0 - lb) * (1.0 - sg)
        logf = jnp.log(f)
        hi = logf.astype(BF16)
        r1 = logf - hi.astype(F32)
        mid = r1.astype(BF16)
        lo = (r1 - mid.astype(F32)).astype(BF16)
        parts = _dot(scan_lhs, jnp.concatenate([hi, mid, lo], axis=1))
        sums = (parts[:, :width] + parts[:, width:2 * width]) + parts[:, 2 * width:]
        cum = sums[:rg]
        tot = sums[rg:]
        qd = (qr * jax.nn.sigmoid(qr)) * jnp.exp(cum)
        ki = (k * jnp.exp(-cum)).astype(BF16)
        ke = k * jnp.exp(tot - cum)
        dec = jnp.exp(tot)
        gate = jax.nn.sigmoid(gr)
        outs, new_states = [], []
        for j, hs in enumerate(heads):
            st = states[j]
            a = jnp.where(causal, _dot_nt(qd[:, hs].astype(BF16), ki[:, hs]), 0.0)
            o_intra = _dot(a.astype(BF16), v[:, hs].astype(BF16))
            o_state = []
            for c in range(sup):
                cs = slice(c * chunk, (c + 1) * chunk)
                if nchunk == 1:
                    g = s * sup + c
                    st = s0_ref[g, j].T if has_s0 else zero_state
                o_state.append(_dot_nt(qd[cs, hs].astype(BF16), st.astype(BF16)))
                upd = lax.dot_general(v[cs, hs].astype(BF16), ke[cs, hs].astype(BF16), _TN,
                                      preferred_element_type=F32)
                st = st * dec[c * chunk:c * chunk + 1, hs] + upd
                if nchunk == 1:
                    sout_ref[g, j] = st.T
            o = o_intra + jnp.concatenate(o_state, axis=0)
            outs.append(_rms_rows(o, onorm[:, hs]) * gate[:, hs])
            new_states.append(st)
        o_ref[pl.ds(r0, rg), :] = jnp.concatenate(outs, axis=1).astype(o_ref.dtype)
        return tuple(new_states)

    if nchunk == 1:
        lax.fori_loop(0, ngroups, group_body, (zero_state,) * nh)
    else:
        st0 = tuple(s0_ref[0, j].T if has_s0 else zero_state for j in range(nh))
        final = lax.fori_loop(0, ngroups, group_body, st0)
        for j in range(nh):
            sout_ref[0, j] = final[j].T


def _hgrn(proj, hg_lower, onorm, s0, *, layer, nb, t, seq_per_step, out_dtype):
    n = proj.shape[0]
    assert n == nb * t
    chunk = min(t, HG_CHUNK)
    assert t % chunk == 0 and nb % seq_per_step == 0
    rows = seq_per_step * t
    nl = hg_lower.shape[0]
    hp = HG_HEADS_PER_STEP
    hsteps = HG_HEADS // hp
    width = hp * HG_DK

    def col(seg):
        return pl.BlockSpec((rows, width), lambda i, h: (i, seg * hsteps + h))

    in_specs = [
        pl.BlockSpec((nl, width), lambda i, h: (0, h)),
        col(0), col(1), col(2), col(3),
        pl.BlockSpec((1, width), lambda i, h: (0, h)),
    ]
    args = [hg_lower, proj, proj, proj, proj, onorm]
    state_spec = pl.BlockSpec((seq_per_step, hp, HG_DK, HG_DV), lambda i, h: (i, h, 0, 0))
    if s0 is not None:
        in_specs.append(state_spec)
        args.append(s0)
    kern = functools.partial(_hgrn_kernel, layer=layer, nseq=seq_per_step, nchunk=t // chunk,
                             chunk=chunk, has_s0=s0 is not None)
    return pl.pallas_call(
        kern,
        name="hgrn",
        grid=(nb // seq_per_step, hsteps),
        in_specs=in_specs,
        out_specs=[
            pl.BlockSpec((rows, width), lambda i, h: (i, h)),
            state_spec,
        ],
        out_shape=[
            jax.ShapeDtypeStruct((n, HG_HEADS * HG_DV), out_dtype),
            jax.ShapeDtypeStruct((nb, HG_HEADS, HG_DK, HG_DV), F32),
        ],
        compiler_params=_cparams("arbitrary", "arbitrary"),
    )(*args)


def _prompt_attn_kernel(q_ref, k_ref, wuv_ref, o_ref, qs_ref, s_ref, p_ref, m_ref, l_ref, acc_ref,
                        *, tq, tk):
    qi = pl.program_id(1)
    rows = MLA_HEADS * tq
    nlg = tk // LANES
    rb = ATTN_SOFTMAX_ROWS
    for h in range(MLA_HEADS):
        qs_ref[h * tq:(h + 1) * tq, :] = q_ref[:, h * QK_PAD:(h + 1) * QK_PAD]
    m_ref[...] = jnp.full((rows, LANES), NEG_INF, F32)
    l_ref[...] = jnp.zeros((rows, LANES), F32)
    acc_ref[...] = jnp.zeros((rows, MLA_KV_LORA), F32)
    nfull = (qi * tq) // tk

    def softmax_rows(r, limit):
        rs = pl.ds(pl.multiple_of(r * rb, rb), rb)
        sb = [s_ref[rs, i * LANES:(i + 1) * LANES] for i in range(nlg)]
        if limit is not None:
            tok = lax.rem(r * rb + lax.broadcasted_iota(jnp.int32, (rb, LANES), 0), tq)
            lane = lax.broadcasted_iota(jnp.int32, (rb, LANES), 1)
            sb = [jnp.where(lane + i * LANES - tok <= limit, sb[i], NEG_INF) for i in range(nlg)]
        mx = functools.reduce(jnp.maximum, sb)
        m_prev = m_ref[rs, :]
        m_new = jnp.maximum(m_prev, jnp.max(mx, axis=-1, keepdims=True))
        corr = jnp.exp(m_prev - m_new)
        ps = [jnp.exp(x - m_new) for x in sb]
        l_ref[rs, :] = l_ref[rs, :] * corr + jnp.sum(functools.reduce(jnp.add, ps), axis=-1,
                                                     keepdims=True)
        m_ref[rs, :] = m_new
        for i in range(nlg):
            p_ref[rs, i * LANES:(i + 1) * LANES] = ps[i].astype(BF16)
        for i in range(MLA_KV_LORA // LANES):
            acc_ref[rs, i * LANES:(i + 1) * LANES] = acc_ref[rs, i * LANES:(i + 1) * LANES] * corr

    def block(j, masked):
        kb = k_ref[pl.ds(pl.multiple_of(j * tk, tk), tk), :]
        hk = tk // 2
        s_ref[:, :hk] = _dot_nt(qs_ref[...], kb[:hk])
        s_ref[:, hk:] = _dot_nt(qs_ref[...], kb[hk:])
        limit = qi * tq - j * tk if masked else None

        def sweep(r, carry):
            softmax_rows(r, limit)
            return carry

        lax.fori_loop(0, rows // rb, sweep, 0, unroll=ATTN_SOFTMAX_UNROLL)
        hr = rows // 2
        acc_ref[:hr, :] += _dot(p_ref[:hr, :], kb[:, :MLA_KV_LORA])
        acc_ref[hr:, :] += _dot(p_ref[hr:, :], kb[:, :MLA_KV_LORA])

    def body(j, carry):
        block(j, False)
        return carry

    lax.fori_loop(0, nfull, body, 0)
    block(nfull, True)
    for h in range(MLA_HEADS):
        hs = slice(h * tq, (h + 1) * tq)
        inv = 1.0 / l_ref[hs, :]
        lat = jnp.concatenate([acc_ref[hs, i * LANES:(i + 1) * LANES] * inv
                               for i in range(MLA_KV_LORA // LANES)], axis=1)
        o_ref[:, h * MLA_V:(h + 1) * MLA_V] = _dot(lat.astype(BF16), wuv_ref[h]).astype(o_ref.dtype)


def _prompt_attn(q, kcat, wuv, *, nb, t):
    n = q.shape[0]
    tq = _tile(t, 128)
    tk = _tile(t, 512)
    assert tk % tq == 0
    nq = t // tq
    rows = MLA_HEADS * tq
    return pl.pallas_call(
        functools.partial(_prompt_attn_kernel, tq=tq, tk=tk),
        name="prompt_attn",
        grid=(nb, nq),
        in_specs=[
            pl.BlockSpec((tq, MLA_HEADS * QK_PAD), lambda b, i: (b * nq + i, 0)),
            pl.BlockSpec((t, QK_PAD), lambda b, i: (b, 0)),
            pl.BlockSpec(wuv.shape, lambda b, i: (0, 0, 0)),
        ],
        out_specs=pl.BlockSpec((tq, MLA_HEADS * MLA_V), lambda b, i: (b * nq + i, 0)),
        out_shape=jax.ShapeDtypeStruct((n, MLA_HEADS * MLA_V), BF16),
        scratch_shapes=[
            pltpu.VMEM((rows, QK_PAD), BF16),
            pltpu.VMEM((rows, tk), F32),
            pltpu.VMEM((rows, tk), BF16),
            pltpu.VMEM((rows, LANES), F32),
            pltpu.VMEM((rows, LANES), F32),
            pltpu.VMEM((rows, MLA_KV_LORA), F32),
        ],
        compiler_params=_cparams("arbitrary", "arbitrary"),
    )(q, kcat, wuv)


def _sample_attn_kernel(pt_ref, q_ref, ckvn_ref, krn_ref, wuv_ref, ckv_hbm, krt_hbm, o_ref,
                        kbuf, rbuf, sem, kc_ref, p_ref, qs_ref, knc_ref, knr_ref,
                        *, layer, n_pages, npg, nb, t):
    b = pl.program_id(0)
    ngroups = n_pages // npg
    rows = MLA_HEADS * t
    page = knc_ref.shape[0]

    def group_copies(n):
        slot = lax.rem(n, GATHER_SLOTS)
        copies = []
        for k in range(npg):
            pid = pt_ref[n * npg + k]
            copies.append(pltpu.make_async_copy(
                ckv_hbm.at[layer, pid], kbuf.at[slot, pl.ds(k * page, page), :], sem.at[slot]))
            copies.append(pltpu.make_async_copy(
                krt_hbm.at[layer, pid], rbuf.at[slot, k], sem.at[slot]))
        return copies

    @pl.when(b == 0)
    def _():
        for n in range(GATHER_AHEAD):
            for c in group_copies(n):
                c.start()

    for h in range(MLA_HEADS):
        qs_ref[h * t:(h + 1) * t, :] = q_ref[:, h * QK_PAD:(h + 1) * QK_PAD]
    qb = qs_ref[...].astype(BF16)
    ql = qb[:, :MLA_KV_LORA]
    qr = qb[:, MLA_KV_LORA:MLA_KV_LORA + MLA_ROPE]

    def softmax_step(s, m, l):
        m_new = jnp.maximum(m, jnp.max(s, axis=-1, keepdims=True))
        corr = jnp.exp(m - m_new)
        p = jnp.exp(s - m_new)
        return p.astype(BF16), corr, m_new, l * corr + jnp.sum(p, axis=-1, keepdims=True)

    m = jnp.full((rows, 1), NEG_INF, F32)
    l = jnp.zeros((rows, 1), F32)
    acc = jnp.zeros((rows, MLA_KV_LORA), F32)
    pending = None
    for g in range(ngroups):
        n = b * ngroups + g
        gslot = lax.rem(n, GATHER_SLOTS)
        slot = g % 2 if ngroups % 2 == 0 else lax.rem(n, 2)
        for c in group_copies(n):
            c.wait()
        for c in group_copies(n + GATHER_AHEAD):
            c.start()
        kc_ref[slot] = kbuf[gslot].astype(BF16)
        s_rope = jnp.concatenate(
            [_dot(qr, rbuf[gslot, k].astype(BF16)) for k in range(npg)], axis=1)
        s = _dot_nt(ql, kc_ref[slot]) + s_rope
        p, corr, m, l = softmax_step(s, m, l)
        p_ref[slot] = p
        if pending is not None:
            pslot, pcorr = pending
            acc = acc * pcorr + _dot(p_ref[pslot], kc_ref[pslot])
        pending = (slot, corr)
    pslot, pcorr = pending
    acc = acc * pcorr + _dot(p_ref[pslot], kc_ref[pslot])

    @pl.when(b == nb - 1)
    def _():
        for n in range(GATHER_AHEAD):
            for c in group_copies(nb * ngroups + n):
                c.wait()

    knc_ref[...] = jnp.zeros(knc_ref.shape, F32)
    knr_ref[...] = jnp.zeros(knr_ref.shape, F32)
    knc_ref[:t, :] = ckvn_ref[...]
    knr_ref[:t, :] = krn_ref[...]
    kc = knc_ref[...].astype(BF16)
    s = _dot_nt(ql, kc) + _dot_nt(qr, knr_ref[...].astype(BF16))
    tq = lax.rem(lax.broadcasted_iota(jnp.int32, (rows, page), 0), t)
    kidx = lax.broadcasted_iota(jnp.int32, (rows, page), 1)
    p, corr, m, l = softmax_step(jnp.where(kidx <= tq, s, NEG_INF), m, l)
    acc = acc * corr + _dot(p, kc)
    lat = acc / l
    for h in range(MLA_HEADS):
        o_ref[:, h * MLA_V:(h + 1) * MLA_V] = _dot(
            lat[h * t:(h + 1) * t].astype(BF16), wuv_ref[h]).astype(o_ref.dtype)


def _sample_attn(q, ckv_new, kr_new, wuv, cache_ckv, cache_krope, page_table, *, layer, nb, t):
    n_pages = page_table.shape[1]
    page = cache_ckv.shape[2]
    npg = min(n_pages, SAMPLE_PAGES_PER_GROUP)
    assert n_pages % npg == 0 and t <= page
    rows = MLA_HEADS * t
    pt_flat = page_table.reshape(-1)
    pt_flat = jnp.concatenate([pt_flat, pt_flat[:GATHER_AHEAD * npg]])
    krope_t = jnp.swapaxes(cache_krope, 2, 3)
    in_specs = [
        pl.BlockSpec((t, MLA_HEADS * QK_PAD), lambda b, pt: (b, 0)),
        pl.BlockSpec((t, MLA_KV_LORA), lambda b, pt: (b, 0)),
        pl.BlockSpec((t, MLA_ROPE), lambda b, pt: (b, 0)),
        pl.BlockSpec(wuv.shape, lambda b, pt: (0, 0, 0)),
        pl.BlockSpec(memory_space=pl.ANY),
        pl.BlockSpec(memory_space=pl.ANY),
    ]
    grid_spec = pltpu.PrefetchScalarGridSpec(
        num_scalar_prefetch=1,
        grid=(nb,),
        in_specs=in_specs,
        out_specs=pl.BlockSpec((t, MLA_HEADS * MLA_V), lambda b, pt: (b, 0)),
        scratch_shapes=[
            pltpu.VMEM((GATHER_SLOTS, npg * page, MLA_KV_LORA), F32),
            pltpu.VMEM((GATHER_SLOTS, npg, MLA_ROPE, page), F32),
            pltpu.SemaphoreType.DMA((GATHER_SLOTS,)),
            pltpu.VMEM((2, npg * page, MLA_KV_LORA), BF16),
            pltpu.VMEM((2, rows, npg * page), BF16),
            pltpu.VMEM((rows, QK_PAD), F32),
            pltpu.VMEM((page, MLA_KV_LORA), F32),
            pltpu.VMEM((page, MLA_ROPE), F32),
        ],
    )
    return pl.pallas_call(
        functools.partial(_sample_attn_kernel, layer=layer, n_pages=n_pages, npg=npg, nb=nb, t=t),
        name="sample_attn",
        grid_spec=grid_spec,
        out_shape=jax.ShapeDtypeStruct((nb * t, MLA_HEADS * MLA_V), F32),
        compiler_params=_cparams("arbitrary"),
    )(pt_flat, q, ckv_new, kr_new, wuv, cache_ckv, krope_t)


def _out_proj_kernel(ohg_ref, omla_ref, x_ref, w1_ref, w2_ref, g_ref, x1_ref, h2_ref):
    y = _dot(ohg_ref[...].astype(BF16), w1_ref[...]) + _dot(omla_ref[...].astype(BF16), w2_ref[...])
    x1 = x_ref[...] + y
    x1_ref[...] = x1
    h2_ref[...] = _rms_rows(x1, g_ref[...]).astype(BF16)


def _out_proj(ohg, omla, x, w1, w2, g):
    n, d = x.shape
    tm = _tile(n, 512)
    row = lambda i: (i, 0)
    fixed = lambda i: (0, 0)
    return pl.pallas_call(
        _out_proj_kernel,
        name="out_proj",
        grid=(n // tm,),
        in_specs=[
            pl.BlockSpec((tm, ohg.shape[1]), row),
            pl.BlockSpec((tm, omla.shape[1]), row),
            pl.BlockSpec((tm, d), row),
            pl.BlockSpec(w1.shape, fixed),
            pl.BlockSpec(w2.shape, fixed),
            pl.BlockSpec((1, d), fixed),
        ],
        out_specs=[pl.BlockSpec((tm, d), row), pl.BlockSpec((tm, d), row)],
        out_shape=[jax.ShapeDtypeStruct((n, d), F32), jax.ShapeDtypeStruct((n, d), BF16)],
        compiler_params=_cparams("arbitrary"),
    )(ohg, omla, x, w1, w2, g)


def _ffn_kernel(*refs, tm, nc, tiles_per_seq, t_sample):
    if t_sample:
        (h_ref, x_ref, wa_ref, wg_ref, wd_ref, cw_ref, cb_ref, sp_ref, o_ref, tail_ref,
         act_ref) = refs
    else:
        (h_ref, x_ref, wa_ref, wg_ref, wd_ref, cw_ref, cb_ref, o_ref, tail_ref,
         act_ref, halo_ref) = refs
    i = pl.program_id(0)
    c = pl.program_id(1)
    tc = wa_ref.shape[1]

    def up_and_activate():
        h = h_ref[...]
        a = _dot(h, wa_ref[...])
        gate = _dot(h, wg_ref[...])
        r1 = pltpu.roll(a, 1, 0)
        r2 = pltpu.roll(a, 2, 0)
        if t_sample:
            sp = sp_ref[...]
            tt = lax.rem(lax.broadcasted_iota(jnp.int32, (tm, tc), 0), t_sample)
            a1 = jnp.where(tt == 0, pltpu.roll(sp, tm - (t_sample - 1), 0), r1)
            a2 = jnp.where(tt < 2, pltpu.roll(sp, tm - (t_sample - 2), 0), r2)
            tail_ref[...] = a
        else:
            halo = jnp.where(lax.rem(i, tiles_per_seq) == 0, 0.0, halo_ref[c])
            row8 = lax.broadcasted_iota(jnp.int32, (SUBLANES, tc), 0)
            top1 = jnp.where(row8 < 1, pltpu.roll(halo, 1, 0), r1[:SUBLANES])
            top2 = jnp.where(row8 < 2, pltpu.roll(halo, 2, 0), r2[:SUBLANES])
            a1 = jnp.concatenate([top1, r1[SUBLANES:]], axis=0)
            a2 = jnp.concatenate([top2, r2[SUBLANES:]], axis=0)
            halo_ref[c] = a[tm - SUBLANES:]
            tail_ref[0] = a[tm - SUBLANES:]
        cw = cw_ref[...]
        conv = ((cb_ref[...] + cw[0:1] * a2) + cw[1:2] * a1) + cw[2:3] * a
        act_ref[lax.rem(c, 2)] = ((conv * jax.nn.sigmoid(conv)) * gate).astype(BF16)

    def down_previous():
        o_ref[...] += _dot(act_ref[lax.rem(c + 1, 2)], wd_ref[...])

    @pl.when(c == 0)
    def _():
        if not t_sample:
            @pl.when(i == 0)
            def _():
                halo_ref[...] = jnp.zeros(halo_ref.shape, F32)
        o_ref[...] = x_ref[...]
        up_and_activate()

    @pl.when(jnp.logical_and(c > 0, c < nc))
    def _():
        up_and_activate()
        down_previous()

    @pl.when(c == nc)
    def _():
        down_previous()


def _ffn(h2, x1, w_up, w_down, conv_w, conv_b, sp, *, t, t_sample):
    n, d = x1.shape
    dff = w_down.shape[0]
    tc = FFN_CHUNK
    nc = dff // tc
    assert dff % tc == 0
    up = lambda c: jnp.minimum(c, nc - 1)
    if t_sample:
        tm = _tile(n, 512)
        assert tm % t_sample == 0
        tiles_per_seq = 1
        tail_shape = (n, dff)
        tail_spec = pl.BlockSpec((tm, tc), lambda i, c: (i, up(c)))
    else:
        tm = _tile(t, 512)
        tiles_per_seq = t // tm
        tail_shape = (n // tm, SUBLANES, dff)
        tail_spec = pl.BlockSpec((1, SUBLANES, tc), lambda i, c: (i, 0, up(c)))
    in_specs = [
        pl.BlockSpec((tm, d), lambda i, c: (i, 0)),
        pl.BlockSpec((tm, d), lambda i, c: (i, 0)),
        pl.BlockSpec((d, tc), lambda i, c: (0, up(c))),
        pl.BlockSpec((d, tc), lambda i, c: (0, nc + up(c))),
        pl.BlockSpec((tc, d), lambda i, c: (jnp.maximum(c - 1, 0), 0)),
        pl.BlockSpec((CONV_W, tc), lambda i, c: (0, up(c))),
        pl.BlockSpec((1, tc), lambda i, c: (0, up(c))),
    ]
    args = [h2, x1, w_up, w_up, w_down, conv_w, conv_b]
    scratch = [pltpu.VMEM((2, tm, tc), BF16)]
    if t_sample:
        in_specs.append(pl.BlockSpec((tm, tc), lambda i, c: (i, up(c))))
        args.append(sp)
    else:
        scratch.append(pltpu.VMEM((nc, SUBLANES, tc), F32))
    return pl.pallas_call(
        functools.partial(_ffn_kernel, tm=tm, nc=nc, tiles_per_seq=tiles_per_seq,
                          t_sample=t_sample),
        name="ffn",
        grid=(n // tm, nc + 1),
        in_specs=in_specs,
        out_specs=[pl.BlockSpec((tm, d), lambda i, c: (i, 0)), tail_spec],
        out_shape=[jax.ShapeDtypeStruct((n, d), F32), jax.ShapeDtypeStruct(tail_shape, F32)],
        scratch_shapes=scratch,
        compiler_params=_cparams("arbitrary", "arbitrary"),
    )(*args)


def _ple_kernel(x_ref, p_ref, gn_ref, wg_ref, wp_ref, fn_ref, y_ref, *, final):
    x = x_ref[...]
    hn = _rms_rows(x, gn_ref[...]).astype(BF16)
    gate = jax.nn.sigmoid(_dot(hn, wg_ref[...]))
    x3 = x + gate * _dot(p_ref[...].astype(BF16), wp_ref[...])
    y_ref[...] = _rms_rows(x3, fn_ref[...]) if final else x3


def _ple(x2, p, gn, wg, wp, fn, *, final):
    n, d = x2.shape
    tm = _tile(n, 512)
    row = lambda i: (i, 0)
    fixed = lambda i: (0, 0)
    return pl.pallas_call(
        functools.partial(_ple_kernel, final=final),
        name="ple",
        grid=(n // tm,),
        in_specs=[
            pl.BlockSpec((tm, d), row),
            pl.BlockSpec((tm, p.shape[1]), row),
            pl.BlockSpec((1, d), fixed),
            pl.BlockSpec(wg.shape, fixed),
            pl.BlockSpec(wp.shape, fixed),
            pl.BlockSpec((1, d), fixed),
        ],
        out_specs=pl.BlockSpec((tm, d), row),
        out_shape=jax.ShapeDtypeStruct((n, d), F32),
        compiler_params=_cparams("arbitrary"),
    )(x2, p, gn, wg, wp, fn)


def _rot_cols(w):
    half = MLA_ROPE // 2
    return jnp.concatenate([-w[..., half:], w[..., :half]], axis=-1)


def _rope_table(pos):
    half = MLA_ROPE // 2
    inv = 1.0 / (ROPE_THETA ** (jnp.arange(half, dtype=F32) / half))
    ang = pos.astype(F32)[:, None] * inv[None, :]
    c, s = jnp.cos(ang), jnp.sin(ang)
    return jnp.concatenate([c, c, s, s], axis=-1)


def _layer_weights(l, norm_mix, w_in, hg_onorm, mla_q_norm, w_q_b, mla_kv_norm, w_kv_b, w_out,
                   norm_ffn, w_up, conv_w, conv_b, w_down, norm_ple, w_ple_gate, w_ple_proj):
    d = w_in.shape[1]
    wit = jnp.swapaxes(w_in[l], 0, 1)
    kr_rows = wit[HG_COLS + MLA_Q_LORA + MLA_KV_LORA:]
    kr_rot = jnp.swapaxes(_rot_cols(jnp.swapaxes(kr_rows, 0, 1)), 0, 1)
    pad = MLA_COLS - (MLA_Q_LORA + MLA_KV_LORA + 2 * MLA_ROPE)
    w_in_ext = jnp.concatenate([wit, kr_rot, jnp.zeros((pad, d), wit.dtype)],
                               axis=0).astype(BF16)
    wq = w_q_b[l].reshape(MLA_Q_LORA, MLA_HEADS, MLA_NOPE + MLA_ROPE)
    wq_ext = jnp.concatenate([wq, _rot_cols(wq[..., MLA_NOPE:])], axis=-1)
    wq_ext = wq_ext.reshape(MLA_Q_LORA, MLA_HEADS * Q_HEAD_COLS).astype(BF16)
    wkv = w_kv_b[l].reshape(MLA_KV_LORA, MLA_HEADS, MLA_NOPE + MLA_V)
    wuk = jnp.transpose(wkv[..., :MLA_NOPE], (1, 2, 0)).astype(BF16)
    wuv = jnp.transpose(wkv[..., MLA_NOPE:], (1, 0, 2)).astype(BF16)
    wo = w_out[l].astype(BF16)
    hgw = HG_HEADS * HG_DV
    return dict(
        norm_mix=norm_mix[l][None], w_in=w_in_ext, hg_onorm=hg_onorm[l][None],
        q_norm=mla_q_norm[l][None], wq=wq_ext, kv_norm=mla_kv_norm[l][None], wuk=wuk, wuv=wuv,
        wo_hg=wo[:hgw], wo_mla=wo[hgw:], norm_ffn=norm_ffn[l][None], w_up=w_up[l].astype(BF16),
        conv_w=conv_w[l], conv_b=conv_b[l][None], w_down=w_down[l].astype(BF16),
        norm_ple=norm_ple[l][None], w_ple_gate=w_ple_gate[l].astype(BF16),
        w_ple_proj=w_ple_proj[l].astype(BF16))


def _trunk(x, p_l, w, hg_lower, norm_final, cs, *, layer, nb, t, final, hg_s0, attend, sample):
    proj = _in_proj(x, w["norm_mix"], w["w_in"])
    ckv, kr, kcat, q = _mla_proj(proj, cs, w["q_norm"], w["kv_norm"], w["wq"], w["wuk"],
                                 F32 if sample else BF16)
    o_hg, hg_s = _hgrn(proj, hg_lower, w["hg_onorm"], hg_s0, layer=layer, nb=nb, t=t,
                       seq_per_step=min(nb, 16) if sample else 1,
                       out_dtype=F32 if sample else BF16)
    o_mla = attend(q, ckv, kr, kcat, w["wuv"])
    x1, h2 = _out_proj(o_hg, o_mla, x, w["wo_hg"], w["wo_mla"], w["norm_ffn"])
    return x1, h2, ckv, kr, hg_s


def kernel(x_prompt, x_sample, cache_ckv, cache_krope, state_hgrn, state_conv, page_table,
           p_prompt, p_sample, norm_mix, w_in, hg_lower, hg_onorm, mla_q_norm, w_q_b,
           mla_kv_norm, w_kv_b, w_out, norm_ffn, w_up, conv_w, conv_b, w_down, norm_ple,
           w_ple_gate, w_ple_proj, norm_final):
    depth = w_in.shape[0]
    bp, tp, d = x_prompt.shape
    bs, ts, _ = x_sample.shape
    dff = w_down.shape[1]
    cs_p = _rope_table(jnp.arange(tp, dtype=jnp.int32))
    cs_s = jnp.tile(_rope_table(PAST_LEN + jnp.arange(ts, dtype=jnp.int32)), (bs, 1))
    nf = norm_final[None]
    xp = x_prompt.reshape(bp * tp, d)
    xs = x_sample.reshape(bs * ts, d)
    outs_p, outs_s = [], []
    for l in range(depth):
        w = _layer_weights(l, norm_mix, w_in, hg_onorm, mla_q_norm, w_q_b, mla_kv_norm, w_kv_b,
                           w_out, norm_ffn, w_up, conv_w, conv_b, w_down, norm_ple, w_ple_gate,
                           w_ple_proj)
        final = l == depth - 1

        attend_p = lambda q, ckv, kr, kcat, wuv: _prompt_attn(q, kcat, wuv, nb=bp, t=tp)
        x1, h2, ckv_p, kr_p, hg_p = _trunk(xp, None, w, hg_lower, nf, cs_p, layer=l, nb=bp, t=tp,
                                           final=final, hg_s0=None, attend=attend_p, sample=False)
        x2, tail = _ffn(h2, x1, w["w_up"], w["w_down"], w["conv_w"], w["conv_b"], None,
                        t=tp, t_sample=0)
        cv_p = tail.reshape(bp, -1, SUBLANES, dff)[:, -1, SUBLANES - (CONV_W - 1):]
        xp = _ple(x2, p_prompt[l].reshape(bp * tp, -1), w["norm_ple"], w["w_ple_gate"],
                  w["w_ple_proj"], nf, final=final)
        outs_p.append((ckv_p.reshape(bp, tp, -1), kr_p.reshape(bp, tp, -1), hg_p, cv_p))

        attend_s = lambda q, ckv, kr, kcat, wuv: _sample_attn(
            q, ckv, kr, wuv, cache_ckv, cache_krope, page_table, layer=l, nb=bs, t=ts)
        x1, h2, ckv_s, kr_s, hg_s = _trunk(xs, None, w, hg_lower, nf, cs_s, layer=l, nb=bs, t=ts,
                                           final=final, hg_s0=state_hgrn[l], attend=attend_s,
                                           sample=True)
        sp = jnp.pad(state_conv[l], ((0, 0), (ts - (CONV_W - 1), 0), (0, 0))).reshape(bs * ts, dff)
        x2, a_full = _ffn(h2, x1, w["w_up"], w["w_down"], w["conv_w"], w["conv_b"], sp,
                          t=ts, t_sample=ts)
        cv_s = a_full.reshape(bs, ts, dff)[:, ts - (CONV_W - 1):]
        xs = _ple(x2, p_sample[l].reshape(bs * ts, -1), w["norm_ple"], w["w_ple_gate"],
                  w["w_ple_proj"], nf, final=final)
        outs_s.append((ckv_s.reshape(bs, ts, -1), kr_s.reshape(bs, ts, -1), hg_s, cv_s))

    stack = lambda outs, k: jnp.stack([o[k] for o in outs])
    return (xp.reshape(bp, tp, d), xs.reshape(bs, ts, d),
            stack(outs_p, 0), stack(outs_p, 1), stack(outs_p, 2), stack(outs_p, 3),
            stack(outs_s, 0), stack(outs_s, 1), stack(outs_s, 2), stack(outs_s, 3))
```

```python
import functools

import jax
import jax.numpy as jnp
from jax import lax
from jax.experimental import pallas as pl
from jax.experimental.pallas import tpu as pltpu

F32 = jnp.float32
BF16 = jnp.bfloat16

HG_HEADS = 8
HG_DK = 128
HG_DV = 128
HG_CHUNK = 64
HG_HEADS_PER_STEP = 1
HG_GROUP_ROWS = 256
IN_PROJ_COLS = 1024
FFN_CHUNK = 512
SAMPLE_PAGES_PER_GROUP = 64
GATHER_AHEAD = 2
GATHER_SLOTS = GATHER_AHEAD + 1
ATTN_SOFTMAX_ROWS = 32
ATTN_SOFTMAX_UNROLL = 32
MLA_HEADS = 8
MLA_Q_LORA = 512
MLA_KV_LORA = 256
MLA_NOPE = 128
MLA_ROPE = 64
MLA_V = 128
MLA_SCALE = (MLA_NOPE + MLA_ROPE) ** -0.5
ROPE_THETA = 10000.0
PAST_LEN = 16384
CONV_W = 3
EPS = 1e-6
NEG_INF = -1e30

HG_COLS = 4 * HG_HEADS * HG_DK
MLA_COLS = 1024
Q_HEAD_COLS = 256
QK_PAD = 384
LANES = 128
SUBLANES = 8

VMEM_LIMIT = 48 * 1024 * 1024

_NT = (((1,), (1,)), ((), ()))
_TN = (((0,), (0,)), ((), ()))


def _cparams(*sem):
    return pltpu.CompilerParams(dimension_semantics=sem, vmem_limit_bytes=VMEM_LIMIT)


def _tile(n, pref):
    t = min(n, pref)
    assert n % t == 0, (n, pref)
    return t


def _rms_rows(x, g):
    ms = jnp.mean(x * x, axis=-1, keepdims=True)
    return (x * lax.rsqrt(ms + EPS)) * g


def _dot(a, b):
    return jnp.dot(a, b, preferred_element_type=F32)


def _dot_nt(a, b):
    return lax.dot_general(a, b, _NT, preferred_element_type=F32)


def _in_proj_kernel(x_ref, g_ref, w_ref, o_ref, h_ref):
    @pl.when(pl.program_id(1) == 0)
    def _():
        h_ref[...] = _rms_rows(x_ref[...], g_ref[...]).astype(BF16)

    o_ref[...] = _dot_nt(h_ref[...], w_ref[...])


def _in_proj(x, g, w):
    n, d = x.shape
    cols = w.shape[0]
    tm = _tile(n, 1024)
    tn = IN_PROJ_COLS
    return pl.pallas_call(
        _in_proj_kernel,
        name="in_proj",
        grid=(n // tm, cols // tn),
        in_specs=[
            pl.BlockSpec((tm, d), lambda i, j: (i, 0)),
            pl.BlockSpec((1, d), lambda i, j: (0, 0)),
            pl.BlockSpec((tn, d), lambda i, j: (j, 0)),
        ],
        out_specs=pl.BlockSpec((tm, tn), lambda i, j: (i, j)),
        out_shape=jax.ShapeDtypeStruct((n, cols), F32),
        scratch_shapes=[pltpu.VMEM((tm, d), BF16)],
        compiler_params=_cparams("arbitrary", "arbitrary"),
    )(x, g, w)


def _mla_proj_kernel(p_ref, cs_ref, qn_ref, kvn_ref, wq_ref, wuk_ref,
                     ckv_ref, kr_ref, kcat_ref, q_ref):
    p = p_ref[...]
    cs = cs_ref[...]
    tm = p.shape[0]
    lane = lax.broadcasted_iota(jnp.int32, (tm, LANES), 1)
    cq = _rms_rows(p[:, :MLA_Q_LORA], qn_ref[...]).astype(BF16)
    ckv = _rms_rows(p[:, MLA_Q_LORA:MLA_Q_LORA + MLA_KV_LORA], kvn_ref[...])
    ckv_ref[...] = ckv
    kk = p[:, 768:896] * cs
    kro = kk + pltpu.roll(kk, MLA_ROPE, 1)
    kr_ref[...] = kro[:, :MLA_ROPE]
    kcat_ref[:, :MLA_KV_LORA] = ckv.astype(kcat_ref.dtype)
    kcat_ref[:, MLA_KV_LORA:] = jnp.where(lane < MLA_ROPE, kro, 0.0).astype(kcat_ref.dtype)
    q = _dot(cq, wq_ref[...])
    for h in range(MLA_HEADS):
        qh = q[:, h * Q_HEAD_COLS:(h + 1) * Q_HEAD_COLS]
        qlat = _dot(qh[:, :MLA_NOPE].astype(BF16), wuk_ref[h]) * MLA_SCALE
        rr = qh[:, MLA_NOPE:] * cs
        rope = (rr + pltpu.roll(rr, MLA_ROPE, 1)) * MLA_SCALE
        q_ref[:, h * QK_PAD:h * QK_PAD + MLA_KV_LORA] = qlat.astype(q_ref.dtype)
        q_ref[:, h * QK_PAD + MLA_KV_LORA:(h + 1) * QK_PAD] = (
            jnp.where(lane < MLA_ROPE, rope, 0.0).astype(q_ref.dtype))


def _mla_proj(proj, cs, qn, kvn, wq, wuk, q_dtype):
    n = proj.shape[0]
    tm = _tile(n, 256)
    cs_blocks = cs.shape[0] // tm
    mla_blk = HG_COLS // MLA_COLS
    return pl.pallas_call(
        _mla_proj_kernel,
        name="mla_proj",
        grid=(n // tm,),
        in_specs=[
            pl.BlockSpec((tm, MLA_COLS), lambda i: (i, mla_blk)),
            pl.BlockSpec((tm, LANES), lambda i: (i % cs_blocks, 0)),
            pl.BlockSpec((1, MLA_Q_LORA), lambda i: (0, 0)),
            pl.BlockSpec((1, MLA_KV_LORA), lambda i: (0, 0)),
            pl.BlockSpec(wq.shape, lambda i: (0, 0)),
            pl.BlockSpec(wuk.shape, lambda i: (0, 0, 0)),
        ],
        out_specs=[
            pl.BlockSpec((tm, MLA_KV_LORA), lambda i: (i, 0)),
            pl.BlockSpec((tm, MLA_ROPE), lambda i: (i, 0)),
            pl.BlockSpec((tm, QK_PAD), lambda i: (i, 0)),
            pl.BlockSpec((tm, MLA_HEADS * QK_PAD), lambda i: (i, 0)),
        ],
        out_shape=[
            jax.ShapeDtypeStruct((n, MLA_KV_LORA), F32),
            jax.ShapeDtypeStruct((n, MLA_ROPE), F32),
            jax.ShapeDtypeStruct((n, QK_PAD), BF16),
            jax.ShapeDtypeStruct((n, MLA_HEADS * QK_PAD), q_dtype),
        ],
        compiler_params=_cparams("arbitrary"),
    )(proj, cs, qn, kvn, wq, wuk)


def _hgrn_kernel(*refs, layer, nseq, nchunk, chunk, has_s0):
    if has_s0:
        hl_ref, q_ref, f_ref, i_ref, g_ref, on_ref, s0_ref, o_ref, sout_ref = refs
    else:
        hl_ref, q_ref, f_ref, i_ref, g_ref, on_ref, o_ref, sout_ref = refs
        s0_ref = None
    hl = hl_ref[...]
    e = jnp.exp(hl - jnp.max(hl, axis=0, keepdims=True))
    lb = jnp.sum(e[:layer + 1], axis=0, keepdims=True) / jnp.sum(e, axis=0, keepdims=True)
    onorm = on_ref[...]
    width = hl.shape[1]
    nh = width // HG_DK
    heads = [slice(j * HG_DK, (j + 1) * HG_DK) for j in range(nh)]
    zero_state = jnp.zeros((HG_DV, HG_DK), F32)
    sup = min(HG_GROUP_ROWS // chunk, nseq * nchunk)
    rg = sup * chunk
    ngroups = (nseq * nchunk) // sup
    assert (nseq * nchunk) % sup == 0 and (nseq == 1 or nchunk == 1)
    rix = lax.broadcasted_iota(jnp.int32, (rg, rg), 0)
    cix = lax.broadcasted_iota(jnp.int32, (rg, rg), 1)
    same = (rix // chunk) == (cix // chunk)
    causal = jnp.logical_and(same, rix >= cix)
    scan_lhs = jnp.concatenate([causal, same], axis=0).astype(BF16)

    def group_body(s, states):
        r0 = pl.multiple_of(s * rg, rg)
        qr = q_ref[pl.ds(r0, rg), :]
        fx = f_ref[pl.ds(r0, rg), :]
        v = i_ref[pl.ds(r0, rg), :]
        gr = g_ref[pl.ds(r0, rg), :]
        sg = jax.nn.sigmoid(fx)
        f = lb + (1.0 - lb) * sg
        k = (1.0 - lb) * (1.0 - sg)
        logf = jnp.log(f)
        hi = logf.astype(BF16)
        r1 = logf - hi.astype(F32)
        mid = r1.astype(BF16)
        lo = (r1 - mid.astype(F32)).astype(BF16)
        parts = _dot(scan_lhs, jnp.concatenate([hi, mid, lo], axis=1))
        sums = (parts[:, :width] + parts[:, width:2 * width]) + parts[:, 2 * width:]
        cum = sums[:rg]
        tot = sums[rg:]
        qd = (qr * jax.nn.sigmoid(qr)) * jnp.exp(cum)
        ki = (k * jnp.exp(-cum)).astype(BF16)
        ke = k * jnp.exp(tot - cum)
        dec = jnp.exp(tot)
        gate = jax.nn.sigmoid(gr)

        def block_diag(x):
            cid = lax.broadcasted_iota(jnp.int32, (rg, HG_DK), 0) // chunk
            return jnp.concatenate([jnp.where(cid == c, x, 0.0) for c in range(sup)],
                                   axis=1).astype(BF16)

        outs, new_states = [], []
        for j, hs in enumerate(heads):
            st = states[j]
            a = jnp.where(causal, _dot_nt(qd[:, hs].astype(BF16), ki[:, hs]), 0.0)
            o_intra = _dot(a.astype(BF16), v[:, hs].astype(BF16))
            if nchunk == 1:
                upd = _dot(v[:, hs].T.astype(BF16), block_diag(ke[:, hs]))
                starts = []
                for c in range(sup):
                    g = s * sup + c
                    st0 = s0_ref[g, j].T if has_s0 else zero_state
                    starts.append(st0.astype(BF16))
                    sout_ref[g, j] = (st0 * dec[c * chunk:c * chunk + 1, hs]
                                      + upd[:, c * HG_DK:(c + 1) * HG_DK]).T
                o = o_intra + _dot_nt(block_diag(qd[:, hs]), jnp.concatenate(starts, axis=1))
            else:
                o_state = []
                for c in range(sup):
                    cs = slice(c * chunk, (c + 1) * chunk)
                    o_state.append(_dot_nt(qd[cs, hs].astype(BF16), st.astype(BF16)))
                    upd = lax.dot_general(v[cs, hs].astype(BF16), ke[cs, hs].astype(BF16), _TN,
                                          preferred_element_type=F32)
                    st = st * dec[c * chunk:c * chunk + 1, hs] + upd
                o = o_intra + jnp.concatenate(o_state, axis=0)
            outs.append(_rms_rows(o, onorm[:, hs]) * gate[:, hs])
            new_states.append(st)
        o_ref[pl.ds(r0, rg), :] = jnp.concatenate(outs, axis=1).astype(o_ref.dtype)
        return tuple(new_states)

    if nchunk == 1:
        lax.fori_loop(0, ngroups, group_body, (zero_state,) * nh)
    else:
        st0 = tuple(s0_ref[0, j].T if has_s0 else zero_state for j in range(nh))
        final = lax.fori_loop(0, ngroups, group_body, st0)
        for j in range(nh):
            sout_ref[0, j] = final[j].T


def _hgrn(proj, hg_lower, onorm, s0, *, layer, nb, t, seq_per_step, out_dtype):
    n = proj.shape[0]
    assert n == nb * t
    chunk = min(t, HG_CHUNK)
    assert t % chunk == 0 and nb % seq_per_step == 0
    rows = seq_per_step * t
    nl = hg_lower.shape[0]
    hp = HG_HEADS_PER_STEP
    hsteps = HG_HEADS // hp
    width = hp * HG_DK

    def col(seg):
        return pl.BlockSpec((rows, width), lambda i, h: (i, seg * hsteps + h))

    in_specs = [
        pl.BlockSpec((nl, width), lambda i, h: (0, h)),
        col(0), col(1), col(2), col(3),
        pl.BlockSpec((1, width), lambda i, h: (0, h)),
    ]
    args = [hg_lower, proj, proj, proj, proj, onorm]
    state_spec = pl.BlockSpec((seq_per_step, hp, HG_DK, HG_DV), lambda i, h: (i, h, 0, 0))
    if s0 is not None:
        in_specs.append(state_spec)
        args.append(s0)
    kern = functools.partial(_hgrn_kernel, layer=layer, nseq=seq_per_step, nchunk=t // chunk,
                             chunk=chunk, has_s0=s0 is not None)
    return pl.pallas_call(
        kern,
        name="hgrn",
        grid=(nb // seq_per_step, hsteps),
        in_specs=in_specs,
        out_specs=[
            pl.BlockSpec((rows, width), lambda i, h: (i, h)),
            state_spec,
        ],
        out_shape=[
            jax.ShapeDtypeStruct((n, HG_HEADS * HG_DV), out_dtype),
            jax.ShapeDtypeStruct((nb, HG_HEADS, HG_DK, HG_DV), F32),
        ],
        compiler_params=_cparams("arbitrary", "arbitrary"),
    )(*args)


def _prompt_attn_kernel(q_ref, k_ref, wuv_ref, o_ref, qs_ref, s_ref, p_ref, m_ref, l_ref, acc_ref,
                        *, tq, tk):
    qi = pl.program_id(1)
    rows = MLA_HEADS * tq
    nlg = tk // LANES
    rb = ATTN_SOFTMAX_ROWS
    for h in range(MLA_HEADS):
        qs_ref[h * tq:(h + 1) * tq, :] = q_ref[:, h * QK_PAD:(h + 1) * QK_PAD]
    m_ref[...] = jnp.full((rows, LANES), NEG_INF, F32)
    l_ref[...] = jnp.zeros((rows, LANES), F32)
    acc_ref[...] = jnp.zeros((rows, MLA_KV_LORA), F32)
    nfull = (qi * tq) // tk

    def softmax_rows(r, limit):
        rs = pl.ds(pl.multiple_of(r * rb, rb), rb)
        sb = [s_ref[rs, i * LANES:(i + 1) * LANES] for i in range(nlg)]
        if limit is not None:
            tok = lax.rem(r * rb + lax.broadcasted_iota(jnp.int32, (rb, LANES), 0), tq)
            lane = lax.broadcasted_iota(jnp.int32, (rb, LANES), 1)
            sb = [jnp.where(lane + i * LANES - tok <= limit, sb[i], NEG_INF) for i in range(nlg)]
        mx = functools.reduce(jnp.maximum, sb)
        m_prev = m_ref[rs, :]
        m_new = jnp.maximum(m_prev, jnp.max(mx, axis=-1, keepdims=True))
        corr = jnp.exp(m_prev - m_new)
        ps = [jnp.exp(x - m_new) for x in sb]
        l_ref[rs, :] = l_ref[rs, :] * corr + jnp.sum(functools.reduce(jnp.add, ps), axis=-1,
                                                     keepdims=True)
        m_ref[rs, :] = m_new
        for i in range(nlg):
            p_ref[rs, i * LANES:(i + 1) * LANES] = ps[i].astype(BF16)
        for i in range(MLA_KV_LORA // LANES):
            acc_ref[rs, i * LANES:(i + 1) * LANES] = acc_ref[rs, i * LANES:(i + 1) * LANES] * corr

    def block(j, masked):
        kb = k_ref[pl.ds(pl.multiple_of(j * tk, tk), tk), :]
        hk = tk // 2
        s_ref[:, :hk] = _dot_nt(qs_ref[...], kb[:hk])
        s_ref[:, hk:] = _dot_nt(qs_ref[...], kb[hk:])
        limit = qi * tq - j * tk if masked else None

        def sweep(r, carry):
            softmax_rows(r, limit)
            return carry

        lax.fori_loop(0, rows // rb, sweep, 0, unroll=ATTN_SOFTMAX_UNROLL)
        hr = rows // 2
        acc_ref[:hr, :] += _dot(p_ref[:hr, :], kb[:, :MLA_KV_LORA])
        acc_ref[hr:, :] += _dot(p_ref[hr:, :], kb[:, :MLA_KV_LORA])

    def body(j, carry):
        block(j, False)
        return carry

    lax.fori_loop(0, nfull, body, 0)
    block(nfull, True)
    for h in range(MLA_HEADS):
        hs = slice(h * tq, (h + 1) * tq)
        inv = 1.0 / l_ref[hs, :]
        lat = jnp.concatenate([acc_ref[hs, i * LANES:(i + 1) * LANES] * inv
                               for i in range(MLA_KV_LORA // LANES)], axis=1)
        o_ref[:, h * MLA_V:(h + 1) * MLA_V] = _dot(lat.astype(BF16), wuv_ref[h]).astype(o_ref.dtype)


def _prompt_attn(q, kcat, wuv, *, nb, t):
    n = q.shape[0]
    tq = _tile(t, 128)
    tk = _tile(t, 512)
    assert tk % tq == 0
    nq = t // tq
    rows = MLA_HEADS * tq
    return pl.pallas_call(
        functools.partial(_prompt_attn_kernel, tq=tq, tk=tk),
        name="prompt_attn",
        grid=(nb, nq),
        in_specs=[
            pl.BlockSpec((tq, MLA_HEADS * QK_PAD), lambda b, i: (b * nq + i, 0)),
            pl.BlockSpec((t, QK_PAD), lambda b, i: (b, 0)),
            pl.BlockSpec(wuv.shape, lambda b, i: (0, 0, 0)),
        ],
        out_specs=pl.BlockSpec((tq, MLA_HEADS * MLA_V), lambda b, i: (b * nq + i, 0)),
        out_shape=jax.ShapeDtypeStruct((n, MLA_HEADS * MLA_V), BF16),
        scratch_shapes=[
            pltpu.VMEM((rows, QK_PAD), BF16),
            pltpu.VMEM((rows, tk), F32),
            pltpu.VMEM((rows, tk), BF16),
            pltpu.VMEM((rows, LANES), F32),
            pltpu.VMEM((rows, LANES), F32),
            pltpu.VMEM((rows, MLA_KV_LORA), F32),
        ],
        compiler_params=_cparams("arbitrary", "arbitrary"),
    )(q, kcat, wuv)


def _sample_attn_kernel(pt_ref, q_ref, ckvn_ref, krn_ref, wuv_ref, ckv_hbm, krt_hbm, o_ref,
                        kbuf, rbuf, sem, kc_ref, p_ref, qs_ref, knc_ref, knr_ref,
                        *, layer, n_pages, npg, nb, t):
    b = pl.program_id(0)
    ngroups = n_pages // npg
    rows = MLA_HEADS * t
    page = knc_ref.shape[0]

    def group_copies(n):
        slot = lax.rem(n, GATHER_SLOTS)
        copies = []
        for k in range(npg):
            pid = pt_ref[n * npg + k]
            copies.append(pltpu.make_async_copy(
                ckv_hbm.at[layer, pid], kbuf.at[slot, pl.ds(k * page, page), :], sem.at[slot]))
            copies.append(pltpu.make_async_copy(
                krt_hbm.at[layer, pid], rbuf.at[slot, k], sem.at[slot]))
        return copies

    @pl.when(b == 0)
    def _():
        for n in range(GATHER_AHEAD):
            for c in group_copies(n):
                c.start()

    for h in range(MLA_HEADS):
        qs_ref[h * t:(h + 1) * t, :] = q_ref[:, h * QK_PAD:(h + 1) * QK_PAD]
    qb = qs_ref[...].astype(BF16)
    ql = qb[:, :MLA_KV_LORA]
    qr = qb[:, MLA_KV_LORA:MLA_KV_LORA + MLA_ROPE]

    def softmax_step(s, m, l):
        m_new = jnp.maximum(m, jnp.max(s, axis=-1, keepdims=True))
        corr = jnp.exp(m - m_new)
        p = jnp.exp(s - m_new)
        return p.astype(BF16), corr, m_new, l * corr + jnp.sum(p, axis=-1, keepdims=True)

    m = jnp.full((rows, 1), NEG_INF, F32)
    l = jnp.zeros((rows, 1), F32)
    acc = jnp.zeros((rows, MLA_KV_LORA), F32)
    pending = None
    for g in range(ngroups):
        n = b * ngroups + g
        gslot = lax.rem(n, GATHER_SLOTS)
        slot = g % 2 if ngroups % 2 == 0 else lax.rem(n, 2)
        for c in group_copies(n):
            c.wait()
        for c in group_copies(n + GATHER_AHEAD):
            c.start()
        kc_ref[slot] = kbuf[gslot].astype(BF16)
        s_rope = jnp.concatenate(
            [_dot(qr, rbuf[gslot, k].astype(BF16)) for k in range(npg)], axis=1)
        s = _dot_nt(ql, kc_ref[slot]) + s_rope
        p, corr, m, l = softmax_step(s, m, l)
        p_ref[slot] = p
        if pending is not None:
            pslot, pcorr = pending
            acc = acc * pcorr + _dot(p_ref[pslot], kc_ref[pslot])
        pending = (slot, corr)
    pslot, pcorr = pending
    acc = acc * pcorr + _dot(p_ref[pslot], kc_ref[pslot])

    @pl.when(b == nb - 1)
    def _():
        for n in range(GATHER_AHEAD):
            for c in group_copies(nb * ngroups + n):
                c.wait()

    knc_ref[...] = jnp.zeros(knc_ref.shape, F32)
    knr_ref[...] = jnp.zeros(knr_ref.shape, F32)
    knc_ref[:t, :] = ckvn_ref[...]
    knr_ref[:t, :] = krn_ref[...]
    kc = knc_ref[...].astype(BF16)
    s = _dot_nt(ql, kc) + _dot_nt(qr, knr_ref[...].astype(BF16))
    tq = lax.rem(lax.broadcasted_iota(jnp.int32, (rows, page), 0), t)
    kidx = lax.broadcasted_iota(jnp.int32, (rows, page), 1)
    p, corr, m, l = softmax_step(jnp.where(kidx <= tq, s, NEG_INF), m, l)
    acc = acc * corr + _dot(p, kc)
    lat = acc / l
    for h in range(MLA_HEADS):
        o_ref[:, h * MLA_V:(h + 1) * MLA_V] = _dot(
            lat[h * t:(h + 1) * t].astype(BF16), wuv_ref[h]).astype(o_ref.dtype)


def _sample_attn(q, ckv_new, kr_new, wuv, cache_ckv, cache_krope, page_table, *, layer, nb, t):
    n_pages = page_table.shape[1]
    page = cache_ckv.shape[2]
    npg = min(n_pages, SAMPLE_PAGES_PER_GROUP)
    assert n_pages % npg == 0 and t <= page
    rows = MLA_HEADS * t
    pt_flat = page_table.reshape(-1)
    pt_flat = jnp.concatenate([pt_flat, pt_flat[:GATHER_AHEAD * npg]])
    krope_t = jnp.swapaxes(cache_krope, 2, 3)
    in_specs = [
        pl.BlockSpec((t, MLA_HEADS * QK_PAD), lambda b, pt: (b, 0)),
        pl.BlockSpec((t, MLA_KV_LORA), lambda b, pt: (b, 0)),
        pl.BlockSpec((t, MLA_ROPE), lambda b, pt: (b, 0)),
        pl.BlockSpec(wuv.shape, lambda b, pt: (0, 0, 0)),
        pl.BlockSpec(memory_space=pl.ANY),
        pl.BlockSpec(memory_space=pl.ANY),
    ]
    grid_spec = pltpu.PrefetchScalarGridSpec(
        num_scalar_prefetch=1,
        grid=(nb,),
        in_specs=in_specs,
        out_specs=pl.BlockSpec((t, MLA_HEADS * MLA_V), lambda b, pt: (b, 0)),
        scratch_shapes=[
            pltpu.VMEM((GATHER_SLOTS, npg * page, MLA_KV_LORA), F32),
            pltpu.VMEM((GATHER_SLOTS, npg, MLA_ROPE, page), F32),
            pltpu.SemaphoreType.DMA((GATHER_SLOTS,)),
            pltpu.VMEM((2, npg * page, MLA_KV_LORA), BF16),
            pltpu.VMEM((2, rows, npg * page), BF16),
            pltpu.VMEM((rows, QK_PAD), F32),
            pltpu.VMEM((page, MLA_KV_LORA), F32),
            pltpu.VMEM((page, MLA_ROPE), F32),
        ],
    )
    return pl.pallas_call(
        functools.partial(_sample_attn_kernel, layer=layer, n_pages=n_pages, npg=npg, nb=nb, t=t),
        name="sample_attn",
        grid_spec=grid_spec,
        out_shape=jax.ShapeDtypeStruct((nb * t, MLA_HEADS * MLA_V), F32),
        compiler_params=_cparams("arbitrary"),
    )(pt_flat, q, ckv_new, kr_new, wuv, cache_ckv, krope_t)


def _out_proj_kernel(ohg_ref, omla_ref, x_ref, w1_ref, w2_ref, g_ref, x1_ref, h2_ref):
    y = _dot(ohg_ref[...].astype(BF16), w1_ref[...]) + _dot(omla_ref[...].astype(BF16), w2_ref[...])
    x1 = x_ref[...] + y
    x1_ref[...] = x1
    h2_ref[...] = _rms_rows(x1, g_ref[...]).astype(BF16)


def _out_proj(ohg, omla, x, w1, w2, g):
    n, d = x.shape
    tm = _tile(n, 512)
    row = lambda i: (i, 0)
    fixed = lambda i: (0, 0)
    return pl.pallas_call(
        _out_proj_kernel,
        name="out_proj",
        grid=(n // tm,),
        in_specs=[
            pl.BlockSpec((tm, ohg.shape[1]), row),
            pl.BlockSpec((tm, omla.shape[1]), row),
            pl.BlockSpec((tm, d), row),
            pl.BlockSpec(w1.shape, fixed),
            pl.BlockSpec(w2.shape, fixed),
            pl.BlockSpec((1, d), fixed),
        ],
        out_specs=[pl.BlockSpec((tm, d), row), pl.BlockSpec((tm, d), row)],
        out_shape=[jax.ShapeDtypeStruct((n, d), F32), jax.ShapeDtypeStruct((n, d), BF16)],
        compiler_params=_cparams("arbitrary"),
    )(ohg, omla, x, w1, w2, g)


def _ffn_kernel(*refs, tm, nc, tiles_per_seq, t_sample):
    if t_sample:
        (h_ref, x_ref, wa_ref, wg_ref, wd_ref, cw_ref, cb_ref, sp_ref, o_ref, tail_ref,
         act_ref) = refs
    else:
        (h_ref, x_ref, wa_ref, wg_ref, wd_ref, cw_ref, cb_ref, o_ref, tail_ref,
         act_ref, halo_ref) = refs
    i = pl.program_id(0)
    c = pl.program_id(1)
    tc = wa_ref.shape[1]

    def up_and_activate():
        h = h_ref[...]
        a = _dot(h, wa_ref[...])
        gate = _dot(h, wg_ref[...])
        r1 = pltpu.roll(a, 1, 0)
        r2 = pltpu.roll(a, 2, 0)
        if t_sample:
            sp = sp_ref[...]
            tt = lax.rem(lax.broadcasted_iota(jnp.int32, (tm, tc), 0), t_sample)
            a1 = jnp.where(tt == 0, pltpu.roll(sp, tm - (t_sample - 1), 0), r1)
            a2 = jnp.where(tt < 2, pltpu.roll(sp, tm - (t_sample - 2), 0), r2)
            tail_ref[...] = a
        else:
            halo = jnp.where(lax.rem(i, tiles_per_seq) == 0, 0.0, halo_ref[c])
            row8 = lax.broadcasted_iota(jnp.int32, (SUBLANES, tc), 0)
            top1 = jnp.where(row8 < 1, pltpu.roll(halo, 1, 0), r1[:SUBLANES])
            top2 = jnp.where(row8 < 2, pltpu.roll(halo, 2, 0), r2[:SUBLANES])
            a1 = jnp.concatenate([top1, r1[SUBLANES:]], axis=0)
            a2 = jnp.concatenate([top2, r2[SUBLANES:]], axis=0)
            halo_ref[c] = a[tm - SUBLANES:]
            tail_ref[0] = a[tm - SUBLANES:]
        cw = cw_ref[...]
        conv = ((cb_ref[...] + cw[0:1] * a2) + cw[1:2] * a1) + cw[2:3] * a
        act_ref[lax.rem(c, 2)] = ((conv * jax.nn.sigmoid(conv)) * gate).astype(BF16)

    def down_previous():
        o_ref[...] += _dot(act_ref[lax.rem(c + 1, 2)], wd_ref[...])

    @pl.when(c == 0)
    def _():
        if not t_sample:
            @pl.when(i == 0)
            def _():
                halo_ref[...] = jnp.zeros(halo_ref.shape, F32)
        o_ref[...] = x_ref[...]
        up_and_activate()

    @pl.when(jnp.logical_and(c > 0, c < nc))
    def _():
        up_and_activate()
        down_previous()

    @pl.when(c == nc)
    def _():
        down_previous()


def _ffn(h2, x1, w_up, w_down, conv_w, conv_b, sp, *, t, t_sample):
    n, d = x1.shape
    dff = w_down.shape[0]
    tc = FFN_CHUNK
    nc = dff // tc
    assert dff % tc == 0
    up = lambda c: jnp.minimum(c, nc - 1)
    if t_sample:
        tm = _tile(n, 512)
        assert tm % t_sample == 0
        tiles_per_seq = 1
        tail_shape = (n, dff)
        tail_spec = pl.BlockSpec((tm, tc), lambda i, c: (i, up(c)))
    else:
        tm = _tile(t, 512)
        tiles_per_seq = t // tm
        tail_shape = (n // tm, SUBLANES, dff)
        tail_spec = pl.BlockSpec((1, SUBLANES, tc), lambda i, c: (i, 0, up(c)))
    in_specs = [
        pl.BlockSpec((tm, d), lambda i, c: (i, 0)),
        pl.BlockSpec((tm, d), lambda i, c: (i, 0)),
        pl.BlockSpec((d, tc), lambda i, c: (0, up(c))),
        pl.BlockSpec((d, tc), lambda i, c: (0, nc + up(c))),
        pl.BlockSpec((tc, d), lambda i, c: (jnp.maximum(c - 1, 0), 0)),
        pl.BlockSpec((CONV_W, tc), lambda i, c: (0, up(c))),
        pl.BlockSpec((1, tc), lambda i, c: (0, up(c))),
    ]
    args = [h2, x1, w_up, w_up, w_down, conv_w, conv_b]
    scratch = [pltpu.VMEM((2, tm, tc), BF16)]
    if t_sample:
        in_specs.append(pl.BlockSpec((tm, tc), lambda i, c: (i, up(c))))
        args.append(sp)
    else:
        scratch.append(pltpu.VMEM((nc, SUBLANES, tc), F32))
    return pl.pallas_call(
        functools.partial(_ffn_kernel, tm=tm, nc=nc, tiles_per_seq=tiles_per_seq,
                          t_sample=t_sample),
        name="ffn",
        grid=(n // tm, nc + 1),
        in_specs=in_specs,
        out_specs=[pl.BlockSpec((tm, d), lambda i, c: (i, 0)), tail_spec],
        out_shape=[jax.ShapeDtypeStruct((n, d), F32), jax.ShapeDtypeStruct(tail_shape, F32)],
        scratch_shapes=scratch,
        compiler_params=_cparams("arbitrary", "arbitrary"),
    )(*args)


def _ple_kernel(x_ref, p_ref, gn_ref, wg_ref, wp_ref, fn_ref, y_ref, *, final):
    x = x_ref[...]
    hn = _rms_rows(x, gn_ref[...]).astype(BF16)
    gate = jax.nn.sigmoid(_dot(hn, wg_ref[...]))
    x3 = x + gate * _dot(p_ref[...].astype(BF16), wp_ref[...])
    y_ref[...] = _rms_rows(x3, fn_ref[...]) if final else x3


def _ple(x2, p, gn, wg, wp, fn, *, final):
    n, d = x2.shape
    tm = _tile(n, 512)
    row = lambda i: (i, 0)
    fixed = lambda i: (0, 0)
    return pl.pallas_call(
        functools.partial(_ple_kernel, final=final),
        name="ple",
        grid=(n // tm,),
        in_specs=[
            pl.BlockSpec((tm, d), row),
            pl.BlockSpec((tm, p.shape[1]), row),
            pl.BlockSpec((1, d), fixed),
            pl.BlockSpec(wg.shape, fixed),
            pl.BlockSpec(wp.shape, fixed),
            pl.BlockSpec((1, d), fixed),
        ],
        out_specs=pl.BlockSpec((tm, d), row),
        out_shape=jax.ShapeDtypeStruct((n, d), F32),
        compiler_params=_cparams("arbitrary"),
    )(x2, p, gn, wg, wp, fn)


def _rot_cols(w):
    half = MLA_ROPE // 2
    return jnp.concatenate([-w[..., half:], w[..., :half]], axis=-1)


def _rope_table(pos):
    half = MLA_ROPE // 2
    inv = 1.0 / (ROPE_THETA ** (jnp.arange(half, dtype=F32) / half))
    ang = pos.astype(F32)[:, None] * inv[None, :]
    c, s = jnp.cos(ang), jnp.sin(ang)
    return jnp.concatenate([c, c, s, s], axis=-1)


def _layer_weights(l, norm_mix, w_in, hg_onorm, mla_q_norm, w_q_b, mla_kv_norm, w_kv_b, w_out,
                   norm_ffn, w_up, conv_w, conv_b, w_down, norm_ple, w_ple_gate, w_ple_proj):
    d = w_in.shape[1]
    wit = jnp.swapaxes(w_in[l], 0, 1)
    kr_rows = wit[HG_COLS + MLA_Q_LORA + MLA_KV_LORA:]
    kr_rot = jnp.swapaxes(_rot_cols(jnp.swapaxes(kr_rows, 0, 1)), 0, 1)
    pad = MLA_COLS - (MLA_Q_LORA + MLA_KV_LORA + 2 * MLA_ROPE)
    w_in_ext = jnp.concatenate([wit, kr_rot, jnp.zeros((pad, d), wit.dtype)],
                               axis=0).astype(BF16)
    wq = w_q_b[l].reshape(MLA_Q_LORA, MLA_HEADS, MLA_NOPE + MLA_ROPE)
    wq_ext = jnp.concatenate([wq, _rot_cols(wq[..., MLA_NOPE:])], axis=-1)
    wq_ext = wq_ext.reshape(MLA_Q_LORA, MLA_HEADS * Q_HEAD_COLS).astype(BF16)
    wkv = w_kv_b[l].reshape(MLA_KV_LORA, MLA_HEADS, MLA_NOPE + MLA_V)
    wuk = jnp.transpose(wkv[..., :MLA_NOPE], (1, 2, 0)).astype(BF16)
    wuv = jnp.transpose(wkv[..., MLA_NOPE:], (1, 0, 2)).astype(BF16)
    wo = w_out[l].astype(BF16)
    hgw = HG_HEADS * HG_DV
    return dict(
        norm_mix=norm_mix[l][None], w_in=w_in_ext, hg_onorm=hg_onorm[l][None],
        q_norm=mla_q_norm[l][None], wq=wq_ext, kv_norm=mla_kv_norm[l][None], wuk=wuk, wuv=wuv,
        wo_hg=wo[:hgw], wo_mla=wo[hgw:], norm_ffn=norm_ffn[l][None], w_up=w_up[l].astype(BF16),
        conv_w=conv_w[l], conv_b=conv_b[l][None], w_down=w_down[l].astype(BF16),
        norm_ple=norm_ple[l][None], w_ple_gate=w_ple_gate[l].astype(BF16),
        w_ple_proj=w_ple_proj[l].astype(BF16))


def _trunk(x, p_l, w, hg_lower, norm_final, cs, *, layer, nb, t, final, hg_s0, attend, sample):
    proj = _in_proj(x, w["norm_mix"], w["w_in"])
    ckv, kr, kcat, q = _mla_proj(proj, cs, w["q_norm"], w["kv_norm"], w["wq"], w["wuk"],
                                 F32 if sample else BF16)
    o_hg, hg_s = _hgrn(proj, hg_lower, w["hg_onorm"], hg_s0, layer=layer, nb=nb, t=t,
                       seq_per_step=min(nb, 16) if sample else 1,
                       out_dtype=F32 if sample else BF16)
    o_mla = attend(q, ckv, kr, kcat, w["wuv"])
    x1, h2 = _out_proj(o_hg, o_mla, x, w["wo_hg"], w["wo_mla"], w["norm_ffn"])
    return x1, h2, ckv, kr, hg_s


def kernel(x_prompt, x_sample, cache_ckv, cache_krope, state_hgrn, state_conv, page_table,
           p_prompt, p_sample, norm_mix, w_in, hg_lower, hg_onorm, mla_q_norm, w_q_b,
           mla_kv_norm, w_kv_b, w_out, norm_ffn, w_up, conv_w, conv_b, w_down, norm_ple,
           w_ple_gate, w_ple_proj, norm_final):
    depth = w_in.shape[0]
    bp, tp, d = x_prompt.shape
    bs, ts, _ = x_sample.shape
    dff = w_down.shape[1]
    cs_p = _rope_table(jnp.arange(tp, dtype=jnp.int32))
    cs_s = jnp.tile(_rope_table(PAST_LEN + jnp.arange(ts, dtype=jnp.int32)), (bs, 1))
    nf = norm_final[None]
    xp = x_prompt.reshape(bp * tp, d)
    xs = x_sample.reshape(bs * ts, d)
    outs_p, outs_s = [], []
    for l in range(depth):
        w = _layer_weights(l, norm_mix, w_in, hg_onorm, mla_q_norm, w_q_b, mla_kv_norm, w_kv_b,
                           w_out, norm_ffn, w_up, conv_w, conv_b, w_down, norm_ple, w_ple_gate,
                           w_ple_proj)
        final = l == depth - 1

        attend_p = lambda q, ckv, kr, kcat, wuv: _prompt_attn(q, kcat, wuv, nb=bp, t=tp)
        x1, h2, ckv_p, kr_p, hg_p = _trunk(xp, None, w, hg_lower, nf, cs_p, layer=l, nb=bp, t=tp,
                                           final=final, hg_s0=None, attend=attend_p, sample=False)
        x2, tail = _ffn(h2, x1, w["w_up"], w["w_down"], w["conv_w"], w["conv_b"], None,
                        t=tp, t_sample=0)
        cv_p = tail.reshape(bp, -1, SUBLANES, dff)[:, -1, SUBLANES - (CONV_W - 1):]
        xp = _ple(x2, p_prompt[l].reshape(bp * tp, -1), w["norm_ple"], w["w_ple_gate"],
                  w["w_ple_proj"], nf, final=final)
        outs_p.append((ckv_p.reshape(bp, tp, -1), kr_p.reshape(bp, tp, -1), hg_p, cv_p))

        attend_s = lambda q, ckv, kr, kcat, wuv: _sample_attn(
            q, ckv, kr, wuv, cache_ckv, cache_krope, page_table, layer=l, nb=bs, t=ts)
        x1, h2, ckv_s, kr_s, hg_s = _trunk(xs, None, w, hg_lower, nf, cs_s, layer=l, nb=bs, t=ts,
                                           final=final, hg_s0=state_hgrn[l], attend=attend_s,
                                           sample=True)
        sp = jnp.pad(state_conv[l], ((0, 0), (ts - (CONV_W - 1), 0), (0, 0))).reshape(bs * ts, dff)
        x2, a_full = _ffn(h2, x1, w["w_up"], w["w_down"], w["conv_w"], w["conv_b"], sp,
                          t=ts, t_sample=ts)
        cv_s = a_full.reshape(bs, ts, dff)[:, ts - (CONV_W - 1):]
        xs = _ple(x2, p_sample[l].reshape(bs * ts, -1), w["norm_ple"], w["w_ple_gate"],
                  w["w_ple_proj"], nf, final=final)
        outs_s.append((ckv_s.reshape(bs, ts, -1), kr_s.reshape(bs, ts, -1), hg_s, cv_s))

    stack = lambda outs, k: jnp.stack([o[k] for o in outs])
    return (xp.reshape(bp, tp, d), xs.reshape(bs, ts, d),
            stack(outs_p, 0), stack(outs_p, 1), stack(outs_p, 2), stack(outs_p, 3),
            stack(outs_s, 0), stack(outs_s, 1), stack(outs_s, 2), stack(outs_s, 3))
```

```python
import functools

import jax
import jax.numpy as jnp
from jax import lax
from jax.experimental import pallas as pl
from jax.experimental.pallas import tpu as pltpu

F32 = jnp.float32
BF16 = jnp.bfloat16

HG_HEADS = 8
HG_DK = 128
HG_DV = 128
HG_CHUNK = 64
HG_HEADS_PER_STEP = 1
HG_GROUP_ROWS = 256
ROW_TILE_IN_PROJ = 1024
ROW_TILE_MLA_PROJ = 256
ROW_TILE_OUT_PROJ = 512
ROW_TILE_FFN = 512
ROW_TILE_PLE = 512
ATTN_QUERY_TOKENS = 128
ATTN_KEY_BLOCK = 512
HG_SAMPLE_SEQS_PER_STEP = 16
IN_PROJ_COLS = 1024
FFN_CHUNK = 512
SAMPLE_PAGES_PER_GROUP = 64
GATHER_AHEAD = 2
GATHER_SLOTS = GATHER_AHEAD + 1
ATTN_SOFTMAX_ROWS = 32
ATTN_SOFTMAX_UNROLL = 32
MLA_HEADS = 8
MLA_Q_LORA = 512
MLA_KV_LORA = 256
MLA_NOPE = 128
MLA_ROPE = 64
MLA_V = 128
MLA_SCALE = (MLA_NOPE + MLA_ROPE) ** -0.5
ROPE_THETA = 10000.0
PAST_LEN = 16384
CONV_W = 3
EPS = 1e-6
NEG_INF = -1e30

HG_COLS = 4 * HG_HEADS * HG_DK
MLA_COLS = 1024
Q_HEAD_COLS = 256
QK_PAD = 384
LANES = 128
SUBLANES = 8

VMEM_LIMIT = 48 * 1024 * 1024

_NT = (((1,), (1,)), ((), ()))
_TN = (((0,), (0,)), ((), ()))


def _cparams(*sem):
    return pltpu.CompilerParams(dimension_semantics=sem, vmem_limit_bytes=VMEM_LIMIT)


def _tile(n, pref):
    t = min(n, pref)
    assert n % t == 0, (n, pref)
    return t


def _rms_rows(x, g):
    ms = jnp.mean(x * x, axis=-1, keepdims=True)
    return (x * lax.rsqrt(ms + EPS)) * g


def _dot(a, b):
    return jnp.dot(a, b, preferred_element_type=F32)


def _dot_nt(a, b):
    return lax.dot_general(a, b, _NT, preferred_element_type=F32)


def _in_proj_kernel(x_ref, g_ref, w_ref, o_ref, h_ref):
    @pl.when(pl.program_id(1) == 0)
    def _():
        h_ref[...] = _rms_rows(x_ref[...], g_ref[...]).astype(BF16)

    o_ref[...] = _dot_nt(h_ref[...], w_ref[...])


def _in_proj(x, g, w):
    n, d = x.shape
    cols = w.shape[0]
    tm = _tile(n, ROW_TILE_IN_PROJ)
    tn = IN_PROJ_COLS
    return pl.pallas_call(
        _in_proj_kernel,
        name="in_proj",
        grid=(n // tm, cols // tn),
        in_specs=[
            pl.BlockSpec((tm, d), lambda i, j: (i, 0)),
            pl.BlockSpec((1, d), lambda i, j: (0, 0)),
            pl.BlockSpec((tn, d), lambda i, j: (j, 0)),
        ],
        out_specs=pl.BlockSpec((tm, tn), lambda i, j: (i, j)),
        out_shape=jax.ShapeDtypeStruct((n, cols), F32),
        scratch_shapes=[pltpu.VMEM((tm, d), BF16)],
        compiler_params=_cparams("arbitrary", "arbitrary"),
    )(x, g, w)


def _mla_proj_kernel(p_ref, cs_ref, qn_ref, kvn_ref, wq_ref, wuk_ref,
                     ckv_ref, kr_ref, kcat_ref, q_ref):
    p = p_ref[...]
    cs = cs_ref[...]
    tm = p.shape[0]
    lane = lax.broadcasted_iota(jnp.int32, (tm, LANES), 1)
    cq = _rms_rows(p[:, :MLA_Q_LORA], qn_ref[...]).astype(BF16)
    ckv = _rms_rows(p[:, MLA_Q_LORA:MLA_Q_LORA + MLA_KV_LORA], kvn_ref[...])
    ckv_ref[...] = ckv
    kk = p[:, 768:896] * cs
    kro = kk + pltpu.roll(kk, MLA_ROPE, 1)
    kr_ref[...] = kro[:, :MLA_ROPE]
    kcat_ref[:, :MLA_KV_LORA] = ckv.astype(kcat_ref.dtype)
    kcat_ref[:, MLA_KV_LORA:] = jnp.where(lane < MLA_ROPE, kro, 0.0).astype(kcat_ref.dtype)
    q = _dot(cq, wq_ref[...])
    for h in range(MLA_HEADS):
        qh = q[:, h * Q_HEAD_COLS:(h + 1) * Q_HEAD_COLS]
        qlat = _dot(qh[:, :MLA_NOPE].astype(BF16), wuk_ref[h]) * MLA_SCALE
        rr = qh[:, MLA_NOPE:] * cs
        rope = (rr + pltpu.roll(rr, MLA_ROPE, 1)) * MLA_SCALE
        q_ref[:, h * QK_PAD:h * QK_PAD + MLA_KV_LORA] = qlat.astype(q_ref.dtype)
        q_ref[:, h * QK_PAD + MLA_KV_LORA:(h + 1) * QK_PAD] = (
            jnp.where(lane < MLA_ROPE, rope, 0.0).astype(q_ref.dtype))


def _mla_proj(proj, cs, qn, kvn, wq, wuk, q_dtype):
    n = proj.shape[0]
    tm = _tile(n, ROW_TILE_MLA_PROJ)
    cs_blocks = cs.shape[0] // tm
    mla_blk = HG_COLS // MLA_COLS
    return pl.pallas_call(
        _mla_proj_kernel,
        name="mla_proj",
        grid=(n // tm,),
        in_specs=[
            pl.BlockSpec((tm, MLA_COLS), lambda i: (i, mla_blk)),
            pl.BlockSpec((tm, LANES), lambda i: (i % cs_blocks, 0)),
            pl.BlockSpec((1, MLA_Q_LORA), lambda i: (0, 0)),
            pl.BlockSpec((1, MLA_KV_LORA), lambda i: (0, 0)),
            pl.BlockSpec(wq.shape, lambda i: (0, 0)),
            pl.BlockSpec(wuk.shape, lambda i: (0, 0, 0)),
        ],
        out_specs=[
            pl.BlockSpec((tm, MLA_KV_LORA), lambda i: (i, 0)),
            pl.BlockSpec((tm, MLA_ROPE), lambda i: (i, 0)),
            pl.BlockSpec((tm, QK_PAD), lambda i: (i, 0)),
            pl.BlockSpec((tm, MLA_HEADS * QK_PAD), lambda i: (i, 0)),
        ],
        out_shape=[
            jax.ShapeDtypeStruct((n, MLA_KV_LORA), F32),
            jax.ShapeDtypeStruct((n, MLA_ROPE), F32),
            jax.ShapeDtypeStruct((n, QK_PAD), BF16),
            jax.ShapeDtypeStruct((n, MLA_HEADS * QK_PAD), q_dtype),
        ],
        compiler_params=_cparams("arbitrary"),
    )(proj, cs, qn, kvn, wq, wuk)


def _hgrn_kernel(*refs, layer, nseq, nchunk, chunk, has_s0):
    if has_s0:
        hl_ref, q_ref, f_ref, i_ref, g_ref, on_ref, s0_ref, o_ref, sout_ref = refs
    else:
        hl_ref, q_ref, f_ref, i_ref, g_ref, on_ref, o_ref, sout_ref = refs
        s0_ref = None
    hl = hl_ref[...]
    e = jnp.exp(hl - jnp.max(hl, axis=0, keepdims=True))
    lb = jnp.sum(e[:layer + 1], axis=0, keepdims=True) / jnp.sum(e, axis=0, keepdims=True)
    onorm = on_ref[...]
    width = hl.shape[1]
    nh = width // HG_DK
    heads = [slice(j * HG_DK, (j + 1) * HG_DK) for j in range(nh)]
    zero_state = jnp.zeros((HG_DV, HG_DK), F32)
    sup = min(HG_GROUP_ROWS // chunk, nseq * nchunk)
    rg = sup * chunk
    ngroups = (nseq * nchunk) // sup
    assert (nseq * nchunk) % sup == 0 and (nseq == 1 or nchunk == 1)
    rix = lax.broadcasted_iota(jnp.int32, (rg, rg), 0)
    cix = lax.broadcasted_iota(jnp.int32, (rg, rg), 1)
    same = (rix // chunk) == (cix // chunk)
    causal = jnp.logical_and(same, rix >= cix)
    scan_lhs = jnp.concatenate([causal, same], axis=0).astype(BF16)

    def group_body(s, states):
        r0 = pl.multiple_of(s * rg, rg)
        qr = q_ref[pl.ds(r0, rg), :]
        fx = f_ref[pl.ds(r0, rg), :]
        v = i_ref[pl.ds(r0, rg), :]
        gr = g_ref[pl.ds(r0, rg), :]
        sg = jax.nn.sigmoid(fx)
        f = lb + (1.0 - lb) * sg
        k = (1.0 - lb) * (1.0 - sg)
        logf = jnp.log(f)
        hi = logf.astype(BF16)
        r1 = logf - hi.astype(F32)
        mid = r1.astype(BF16)
        lo = (r1 - mid.astype(F32)).astype(BF16)
        parts = _dot(scan_lhs, jnp.concatenate([hi, mid, lo], axis=1))
        sums = (parts[:, :width] + parts[:, width:2 * width]) + parts[:, 2 * width:]
        cum = sums[:rg]
        tot = sums[rg:]
        qd = (qr * jax.nn.sigmoid(qr)) * jnp.exp(cum)
        ki = (k * jnp.exp(-cum)).astype(BF16)
        ke = k * jnp.exp(tot - cum)
        dec = jnp.exp(tot)
        gate = jax.nn.sigmoid(gr)

        def block_diag(x):
            cid = lax.broadcasted_iota(jnp.int32, (rg, HG_DK), 0) // chunk
            return jnp.concatenate([jnp.where(cid == c, x, 0.0) for c in range(sup)],
                                   axis=1).astype(BF16)

        outs, new_states = [], []
        for j, hs in enumerate(heads):
            st = states[j]
            a = jnp.where(causal, _dot_nt(qd[:, hs].astype(BF16), ki[:, hs]), 0.0)
            o_intra = _dot(a.astype(BF16), v[:, hs].astype(BF16))
            if nchunk == 1:
                upd = _dot(v[:, hs].T.astype(BF16), block_diag(ke[:, hs]))
                starts = []
                for c in range(sup):
                    g = s * sup + c
                    st0 = s0_ref[g, j].T if has_s0 else zero_state
                    starts.append(st0.astype(BF16))
                    sout_ref[g, j] = (st0 * dec[c * chunk:c * chunk + 1, hs]
                                      + upd[:, c * HG_DK:(c + 1) * HG_DK]).T
                o = o_intra + _dot_nt(block_diag(qd[:, hs]), jnp.concatenate(starts, axis=1))
            else:
                o_state = []
                for c in range(sup):
                    cs = slice(c * chunk, (c + 1) * chunk)
                    o_state.append(_dot_nt(qd[cs, hs].astype(BF16), st.astype(BF16)))
                    upd = lax.dot_general(v[cs, hs].astype(BF16), ke[cs, hs].astype(BF16), _TN,
                                          preferred_element_type=F32)
                    st = st * dec[c * chunk:c * chunk + 1, hs] + upd
                o = o_intra + jnp.concatenate(o_state, axis=0)
            outs.append(_rms_rows(o, onorm[:, hs]) * gate[:, hs])
            new_states.append(st)
        o_ref[pl.ds(r0, rg), :] = jnp.concatenate(outs, axis=1).astype(o_ref.dtype)
        return tuple(new_states)

    if nchunk == 1:
        lax.fori_loop(0, ngroups, group_body, (zero_state,) * nh)
    else:
        st0 = tuple(s0_ref[0, j].T if has_s0 else zero_state for j in range(nh))
        final = lax.fori_loop(0, ngroups, group_body, st0)
        for j in range(nh):
            sout_ref[0, j] = final[j].T


def _hgrn(proj, hg_lower, onorm, s0, *, layer, nb, t, seq_per_step, out_dtype):
    n = proj.shape[0]
    assert n == nb * t
    chunk = min(t, HG_CHUNK)
    assert t % chunk == 0 and nb % seq_per_step == 0
    rows = seq_per_step * t
    nl = hg_lower.shape[0]
    hp = HG_HEADS_PER_STEP
    hsteps = HG_HEADS // hp
    width = hp * HG_DK

    def col(seg):
        return pl.BlockSpec((rows, width), lambda i, h: (i, seg * hsteps + h))

    in_specs = [
        pl.BlockSpec((nl, width), lambda i, h: (0, h)),
        col(0), col(1), col(2), col(3),
        pl.BlockSpec((1, width), lambda i, h: (0, h)),
    ]
    args = [hg_lower, proj, proj, proj, proj, onorm]
    state_spec = pl.BlockSpec((seq_per_step, hp, HG_DK, HG_DV), lambda i, h: (i, h, 0, 0))
    if s0 is not None:
        in_specs.append(state_spec)
        args.append(s0)
    kern = functools.partial(_hgrn_kernel, layer=layer, nseq=seq_per_step, nchunk=t // chunk,
                             chunk=chunk, has_s0=s0 is not None)
    return pl.pallas_call(
        kern,
        name="hgrn",
        grid=(nb // seq_per_step, hsteps),
        in_specs=in_specs,
        out_specs=[
            pl.BlockSpec((rows, width), lambda i, h: (i, h)),
            state_spec,
        ],
        out_shape=[
            jax.ShapeDtypeStruct((n, HG_HEADS * HG_DV), out_dtype),
            jax.ShapeDtypeStruct((nb, HG_HEADS, HG_DK, HG_DV), F32),
        ],
        compiler_params=_cparams("arbitrary", "arbitrary"),
    )(*args)


def _prompt_attn_kernel(q_ref, k_ref, wuv_ref, o_ref, qs_ref, s_ref, p_ref, m_ref, l_ref, acc_ref,
                        *, tq, tk):
    qi = pl.program_id(1)
    rows = MLA_HEADS * tq
    nlg = tk // LANES
    rb = ATTN_SOFTMAX_ROWS
    for h in range(MLA_HEADS):
        qs_ref[h * tq:(h + 1) * tq, :] = q_ref[:, h * QK_PAD:(h + 1) * QK_PAD]
    m_ref[...] = jnp.full((rows, LANES), NEG_INF, F32)
    l_ref[...] = jnp.zeros((rows, LANES), F32)
    acc_ref[...] = jnp.zeros((rows, MLA_KV_LORA), F32)
    nfull = (qi * tq) // tk

    def softmax_rows(r, limit):
        rs = pl.ds(pl.multiple_of(r * rb, rb), rb)
        sb = [s_ref[rs, i * LANES:(i + 1) * LANES] for i in range(nlg)]
        if limit is not None:
            tok = lax.rem(r * rb + lax.broadcasted_iota(jnp.int32, (rb, LANES), 0), tq)
            lane = lax.broadcasted_iota(jnp.int32, (rb, LANES), 1)
            sb = [jnp.where(lane + i * LANES - tok <= limit, sb[i], NEG_INF) for i in range(nlg)]
        mx = functools.reduce(jnp.maximum, sb)
        m_prev = m_ref[rs, :]
        m_new = jnp.maximum(m_prev, jnp.max(mx, axis=-1, keepdims=True))
        corr = jnp.exp(m_prev - m_new)
        ps = [jnp.exp(x - m_new) for x in sb]
        l_ref[rs, :] = l_ref[rs, :] * corr + jnp.sum(functools.reduce(jnp.add, ps), axis=-1,
                                                     keepdims=True)
        m_ref[rs, :] = m_new
        for i in range(nlg):
            p_ref[rs, i * LANES:(i + 1) * LANES] = ps[i].astype(BF16)
        for i in range(MLA_KV_LORA // LANES):
            acc_ref[rs, i * LANES:(i + 1) * LANES] = acc_ref[rs, i * LANES:(i + 1) * LANES] * corr

    def block(j, masked):
        kb = k_ref[pl.ds(pl.multiple_of(j * tk, tk), tk), :]
        hk = tk // 2
        s_ref[:, :hk] = _dot_nt(qs_ref[...], kb[:hk])
        s_ref[:, hk:] = _dot_nt(qs_ref[...], kb[hk:])
        limit = qi * tq - j * tk if masked else None

        def sweep(r, carry):
            softmax_rows(r, limit)
            return carry

        lax.fori_loop(0, rows // rb, sweep, 0, unroll=ATTN_SOFTMAX_UNROLL)
        hr = rows // 2
        acc_ref[:hr, :] += _dot(p_ref[:hr, :], kb[:, :MLA_KV_LORA])
        acc_ref[hr:, :] += _dot(p_ref[hr:, :], kb[:, :MLA_KV_LORA])

    def body(j, carry):
        block(j, False)
        return carry

    lax.fori_loop(0, nfull, body, 0)
    block(nfull, True)
    for h in range(MLA_HEADS):
        hs = slice(h * tq, (h + 1) * tq)
        inv = 1.0 / l_ref[hs, :]
        lat = jnp.concatenate([acc_ref[hs, i * LANES:(i + 1) * LANES] * inv
                               for i in range(MLA_KV_LORA // LANES)], axis=1)
        o_ref[:, h * MLA_V:(h + 1) * MLA_V] = _dot(lat.astype(BF16), wuv_ref[h]).astype(o_ref.dtype)


def _prompt_attn(q, kcat, wuv, *, nb, t):
    n = q.shape[0]
    tq = _tile(t, ATTN_QUERY_TOKENS)
    tk = _tile(t, ATTN_KEY_BLOCK)
    assert tk % tq == 0
    nq = t // tq
    rows = MLA_HEADS * tq
    return pl.pallas_call(
        functools.partial(_prompt_attn_kernel, tq=tq, tk=tk),
        name="prompt_attn",
        grid=(nb, nq),
        in_specs=[
            pl.BlockSpec((tq, MLA_HEADS * QK_PAD), lambda b, i: (b * nq + i, 0)),
            pl.BlockSpec((t, QK_PAD), lambda b, i: (b, 0)),
            pl.BlockSpec(wuv.shape, lambda b, i: (0, 0, 0)),
        ],
        out_specs=pl.BlockSpec((tq, MLA_HEADS * MLA_V), lambda b, i: (b * nq + i, 0)),
        out_shape=jax.ShapeDtypeStruct((n, MLA_HEADS * MLA_V), BF16),
        scratch_shapes=[
            pltpu.VMEM((rows, QK_PAD), BF16),
            pltpu.VMEM((rows, tk), F32),
            pltpu.VMEM((rows, tk), BF16),
            pltpu.VMEM((rows, LANES), F32),
            pltpu.VMEM((rows, LANES), F32),
            pltpu.VMEM((rows, MLA_KV_LORA), F32),
        ],
        compiler_params=_cparams("arbitrary", "arbitrary"),
    )(q, kcat, wuv)


def _sample_attn_kernel(pt_ref, q_ref, ckvn_ref, krn_ref, wuv_ref, ckv_hbm, krt_hbm, o_ref,
                        kbuf, rbuf, sem, kc_ref, p_ref, qs_ref, knc_ref, knr_ref,
                        *, layer, n_pages, npg, nb, t):
    b = pl.program_id(0)
    ngroups = n_pages // npg
    rows = MLA_HEADS * t
    page = knc_ref.shape[0]

    def group_copies(n):
        slot = lax.rem(n, GATHER_SLOTS)
        copies = []
        for k in range(npg):
            pid = pt_ref[n * npg + k]
            copies.append(pltpu.make_async_copy(
                ckv_hbm.at[layer, pid], kbuf.at[slot, pl.ds(k * page, page), :], sem.at[slot]))
            copies.append(pltpu.make_async_copy(
                krt_hbm.at[layer, pid], rbuf.at[slot, k], sem.at[slot]))
        return copies

    @pl.when(b == 0)
    def _():
        for n in range(GATHER_AHEAD):
            for c in group_copies(n):
                c.start()

    for h in range(MLA_HEADS):
        qs_ref[h * t:(h + 1) * t, :] = q_ref[:, h * QK_PAD:(h + 1) * QK_PAD]
    qb = qs_ref[...].astype(BF16)
    ql = qb[:, :MLA_KV_LORA]
    qr = qb[:, MLA_KV_LORA:MLA_KV_LORA + MLA_ROPE]

    def softmax_step(s, m, l):
        m_new = jnp.maximum(m, jnp.max(s, axis=-1, keepdims=True))
        corr = jnp.exp(m - m_new)
        p = jnp.exp(s - m_new)
        return p.astype(BF16), corr, m_new, l * corr + jnp.sum(p, axis=-1, keepdims=True)

    m = jnp.full((rows, 1), NEG_INF, F32)
    l = jnp.zeros((rows, 1), F32)
    acc = jnp.zeros((rows, MLA_KV_LORA), F32)
    pending = None
    for g in range(ngroups):
        n = b * ngroups + g
        gslot = lax.rem(n, GATHER_SLOTS)
        slot = g % 2 if ngroups % 2 == 0 else lax.rem(n, 2)
        for c in group_copies(n):
            c.wait()
        for c in group_copies(n + GATHER_AHEAD):
            c.start()
        kc_ref[slot] = kbuf[gslot].astype(BF16)
        s_rope = jnp.concatenate(
            [_dot(qr, rbuf[gslot, k].astype(BF16)) for k in range(npg)], axis=1)
        s = _dot_nt(ql, kc_ref[slot]) + s_rope
        p, corr, m, l = softmax_step(s, m, l)
        p_ref[slot] = p
        if pending is not None:
            pslot, pcorr = pending
            acc = acc * pcorr + _dot(p_ref[pslot], kc_ref[pslot])
        pending = (slot, corr)
    pslot, pcorr = pending
    acc = acc * pcorr + _dot(p_ref[pslot], kc_ref[pslot])

    @pl.when(b == nb - 1)
    def _():
        for n in range(GATHER_AHEAD):
            for c in group_copies(nb * ngroups + n):
                c.wait()

    knc_ref[...] = jnp.zeros(knc_ref.shape, F32)
    knr_ref[...] = jnp.zeros(knr_ref.shape, F32)
    knc_ref[:t, :] = ckvn_ref[...]
    knr_ref[:t, :] = krn_ref[...]
    kc = knc_ref[...].astype(BF16)
    s = _dot_nt(ql, kc) + _dot_nt(qr, knr_ref[...].astype(BF16))
    tq = lax.rem(lax.broadcasted_iota(jnp.int32, (rows, page), 0), t)
    kidx = lax.broadcasted_iota(jnp.int32, (rows, page), 1)
    p, corr, m, l = softmax_step(jnp.where(kidx <= tq, s, NEG_INF), m, l)
    acc = acc * corr + _dot(p, kc)
    lat = acc / l
    for h in range(MLA_HEADS):
        o_ref[:, h * MLA_V:(h + 1) * MLA_V] = _dot(
            lat[h * t:(h + 1) * t].astype(BF16), wuv_ref[h]).astype(o_ref.dtype)


def _sample_attn(q, ckv_new, kr_new, wuv, cache_ckv, cache_krope, page_table, *, layer, nb, t):
    n_pages = page_table.shape[1]
    page = cache_ckv.shape[2]
    npg = min(n_pages, SAMPLE_PAGES_PER_GROUP)
    assert n_pages % npg == 0 and t <= page
    rows = MLA_HEADS * t
    pt_flat = page_table.reshape(-1)
    pt_flat = jnp.concatenate([pt_flat, pt_flat[:GATHER_AHEAD * npg]])
    krope_t = jnp.swapaxes(cache_krope, 2, 3)
    in_specs = [
        pl.BlockSpec((t, MLA_HEADS * QK_PAD), lambda b, pt: (b, 0)),
        pl.BlockSpec((t, MLA_KV_LORA), lambda b, pt: (b, 0)),
        pl.BlockSpec((t, MLA_ROPE), lambda b, pt: (b, 0)),
        pl.BlockSpec(wuv.shape, lambda b, pt: (0, 0, 0)),
        pl.BlockSpec(memory_space=pl.ANY),
        pl.BlockSpec(memory_space=pl.ANY),
    ]
    grid_spec = pltpu.PrefetchScalarGridSpec(
        num_scalar_prefetch=1,
        grid=(nb,),
        in_specs=in_specs,
        out_specs=pl.BlockSpec((t, MLA_HEADS * MLA_V), lambda b, pt: (b, 0)),
        scratch_shapes=[
            pltpu.VMEM((GATHER_SLOTS, npg * page, MLA_KV_LORA), F32),
            pltpu.VMEM((GATHER_SLOTS, npg, MLA_ROPE, page), F32),
            pltpu.SemaphoreType.DMA((GATHER_SLOTS,)),
            pltpu.VMEM((2, npg * page, MLA_KV_LORA), BF16),
            pltpu.VMEM((2, rows, npg * page), BF16),
            pltpu.VMEM((rows, QK_PAD), F32),
            pltpu.VMEM((page, MLA_KV_LORA), F32),
            pltpu.VMEM((page, MLA_ROPE), F32),
        ],
    )
    return pl.pallas_call(
        functools.partial(_sample_attn_kernel, layer=layer, n_pages=n_pages, npg=npg, nb=nb, t=t),
        name="sample_attn",
        grid_spec=grid_spec,
        out_shape=jax.ShapeDtypeStruct((nb * t, MLA_HEADS * MLA_V), F32),
        compiler_params=_cparams("arbitrary"),
    )(pt_flat, q, ckv_new, kr_new, wuv, cache_ckv, krope_t)


def _out_proj_kernel(ohg_ref, omla_ref, x_ref, w1_ref, w2_ref, g_ref, x1_ref, h2_ref):
    y = _dot(ohg_ref[...].astype(BF16), w1_ref[...]) + _dot(omla_ref[...].astype(BF16), w2_ref[...])
    x1 = x_ref[...] + y
    x1_ref[...] = x1
    h2_ref[...] = _rms_rows(x1, g_ref[...]).astype(BF16)


def _out_proj(ohg, omla, x, w1, w2, g):
    n, d = x.shape
    tm = _tile(n, ROW_TILE_OUT_PROJ)
    row = lambda i: (i, 0)
    fixed = lambda i: (0, 0)
    return pl.pallas_call(
        _out_proj_kernel,
        name="out_proj",
        grid=(n // tm,),
        in_specs=[
            pl.BlockSpec((tm, ohg.shape[1]), row),
            pl.BlockSpec((tm, omla.shape[1]), row),
            pl.BlockSpec((tm, d), row),
            pl.BlockSpec(w1.shape, fixed),
            pl.BlockSpec(w2.shape, fixed),
            pl.BlockSpec((1, d), fixed),
        ],
        out_specs=[pl.BlockSpec((tm, d), row), pl.BlockSpec((tm, d), row)],
        out_shape=[jax.ShapeDtypeStruct((n, d), F32), jax.ShapeDtypeStruct((n, d), BF16)],
        compiler_params=_cparams("arbitrary"),
    )(ohg, omla, x, w1, w2, g)


def _ffn_kernel(*refs, tm, nc, tiles_per_seq, t_sample):
    if t_sample:
        (h_ref, x_ref, wa_ref, wg_ref, wd_ref, cw_ref, cb_ref, sp_ref, o_ref, tail_ref,
         act_ref) = refs
    else:
        (h_ref, x_ref, wa_ref, wg_ref, wd_ref, cw_ref, cb_ref, o_ref, tail_ref,
         act_ref, halo_ref) = refs
    i = pl.program_id(0)
    c = pl.program_id(1)
    tc = wa_ref.shape[1]

    def up_and_activate():
        h = h_ref[...]
        a = _dot(h, wa_ref[...])
        gate = _dot(h, wg_ref[...])
        r1 = pltpu.roll(a, 1, 0)
        r2 = pltpu.roll(a, 2, 0)
        if t_sample:
            sp = sp_ref[...]
            tt = lax.rem(lax.broadcasted_iota(jnp.int32, (tm, tc), 0), t_sample)
            a1 = jnp.where(tt == 0, pltpu.roll(sp, tm - (t_sample - 1), 0), r1)
            a2 = jnp.where(tt < 2, pltpu.roll(sp, tm - (t_sample - 2), 0), r2)
            tail_ref[...] = a
        else:
            halo = jnp.where(lax.rem(i, tiles_per_seq) == 0, 0.0, halo_ref[c])
            row8 = lax.broadcasted_iota(jnp.int32, (SUBLANES, tc), 0)
            top1 = jnp.where(row8 < 1, pltpu.roll(halo, 1, 0), r1[:SUBLANES])
            top2 = jnp.where(row8 < 2, pltpu.roll(halo, 2, 0), r2[:SUBLANES])
            a1 = jnp.concatenate([top1, r1[SUBLANES:]], axis=0)
            a2 = jnp.concatenate([top2, r2[SUBLANES:]], axis=0)
            halo_ref[c] = a[tm - SUBLANES:]
            tail_ref[0] = a[tm - SUBLANES:]
        cw = cw_ref[...]
        conv = ((cb_ref[...] + cw[0:1] * a2) + cw[1:2] * a1) + cw[2:3] * a
        act_ref[lax.rem(c, 2)] = ((conv * jax.nn.sigmoid(conv)) * gate).astype(BF16)

    def down_previous():
        o_ref[...] += _dot(act_ref[lax.rem(c + 1, 2)], wd_ref[...])

    @pl.when(c == 0)
    def _():
        if not t_sample:
            @pl.when(i == 0)
            def _():
                halo_ref[...] = jnp.zeros(halo_ref.shape, F32)
        o_ref[...] = x_ref[...]
        up_and_activate()

    @pl.when(jnp.logical_and(c > 0, c < nc))
    def _():
        up_and_activate()
        down_previous()

    @pl.when(c == nc)
    def _():
        down_previous()


def _ffn(h2, x1, w_up, w_down, conv_w, conv_b, sp, *, t, t_sample):
    n, d = x1.shape
    dff = w_down.shape[0]
    tc = FFN_CHUNK
    nc = dff // tc
    assert dff % tc == 0
    up = lambda c: jnp.minimum(c, nc - 1)
    if t_sample:
        tm = _tile(n, ROW_TILE_FFN)
        assert tm % t_sample == 0
        tiles_per_seq = 1
        tail_shape = (n, dff)
        tail_spec = pl.BlockSpec((tm, tc), lambda i, c: (i, up(c)))
    else:
        tm = _tile(t, ROW_TILE_FFN)
        tiles_per_seq = t // tm
        tail_shape = (n // tm, SUBLANES, dff)
        tail_spec = pl.BlockSpec((1, SUBLANES, tc), lambda i, c: (i, 0, up(c)))
    in_specs = [
        pl.BlockSpec((tm, d), lambda i, c: (i, 0)),
        pl.BlockSpec((tm, d), lambda i, c: (i, 0)),
        pl.BlockSpec((d, tc), lambda i, c: (0, up(c))),
        pl.BlockSpec((d, tc), lambda i, c: (0, nc + up(c))),
        pl.BlockSpec((tc, d), lambda i, c: (jnp.maximum(c - 1, 0), 0)),
        pl.BlockSpec((CONV_W, tc), lambda i, c: (0, up(c))),
        pl.BlockSpec((1, tc), lambda i, c: (0, up(c))),
    ]
    args = [h2, x1, w_up, w_up, w_down, conv_w, conv_b]
    scratch = [pltpu.VMEM((2, tm, tc), BF16)]
    if t_sample:
        in_specs.append(pl.BlockSpec((tm, tc), lambda i, c: (i, up(c))))
        args.append(sp)
    else:
        scratch.append(pltpu.VMEM((nc, SUBLANES, tc), F32))
    return pl.pallas_call(
        functools.partial(_ffn_kernel, tm=tm, nc=nc, tiles_per_seq=tiles_per_seq,
                          t_sample=t_sample),
        name="ffn",
        grid=(n // tm, nc + 1),
        in_specs=in_specs,
        out_specs=[pl.BlockSpec((tm, d), lambda i, c: (i, 0)), tail_spec],
        out_shape=[jax.ShapeDtypeStruct((n, d), F32), jax.ShapeDtypeStruct(tail_shape, F32)],
        scratch_shapes=scratch,
        compiler_params=_cparams("arbitrary", "arbitrary"),
    )(*args)


def _ple_kernel(x_ref, p_ref, gn_ref, wg_ref, wp_ref, fn_ref, y_ref, *, final):
    x = x_ref[...]
    hn = _rms_rows(x, gn_ref[...]).astype(BF16)
    gate = jax.nn.sigmoid(_dot(hn, wg_ref[...]))
    x3 = x + gate * _dot(p_ref[...].astype(BF16), wp_ref[...])
    y_ref[...] = _rms_rows(x3, fn_ref[...]) if final else x3


def _ple(x2, p, gn, wg, wp, fn, *, final):
    n, d = x2.shape
    tm = _tile(n, ROW_TILE_PLE)
    row = lambda i: (i, 0)
    fixed = lambda i: (0, 0)
    return pl.pallas_call(
        functools.partial(_ple_kernel, final=final),
        name="ple",
        grid=(n // tm,),
        in_specs=[
            pl.BlockSpec((tm, d), row),
            pl.BlockSpec((tm, p.shape[1]), row),
            pl.BlockSpec((1, d), fixed),
            pl.BlockSpec(wg.shape, fixed),
            pl.BlockSpec(wp.shape, fixed),
            pl.BlockSpec((1, d), fixed),
        ],
        out_specs=pl.BlockSpec((tm, d), row),
        out_shape=jax.ShapeDtypeStruct((n, d), F32),
        compiler_params=_cparams("arbitrary"),
    )(x2, p, gn, wg, wp, fn)


def _rot_cols(w):
    half = MLA_ROPE // 2
    return jnp.concatenate([-w[..., half:], w[..., :half]], axis=-1)


def _rope_table(pos):
    half = MLA_ROPE // 2
    inv = 1.0 / (ROPE_THETA ** (jnp.arange(half, dtype=F32) / half))
    ang = pos.astype(F32)[:, None] * inv[None, :]
    c, s = jnp.cos(ang), jnp.sin(ang)
    return jnp.concatenate([c, c, s, s], axis=-1)


def _layer_weights(l, norm_mix, w_in, hg_onorm, mla_q_norm, w_q_b, mla_kv_norm, w_kv_b, w_out,
                   norm_ffn, w_up, conv_w, conv_b, w_down, norm_ple, w_ple_gate, w_ple_proj):
    d = w_in.shape[1]
    wit = jnp.swapaxes(w_in[l], 0, 1)
    kr_rows = wit[HG_COLS + MLA_Q_LORA + MLA_KV_LORA:]
    kr_rot = jnp.swapaxes(_rot_cols(jnp.swapaxes(kr_rows, 0, 1)), 0, 1)
    pad = MLA_COLS - (MLA_Q_LORA + MLA_KV_LORA + 2 * MLA_ROPE)
    w_in_ext = jnp.concatenate([wit, kr_rot, jnp.zeros((pad, d), wit.dtype)],
                               axis=0).astype(BF16)
    wq = w_q_b[l].reshape(MLA_Q_LORA, MLA_HEADS, MLA_NOPE + MLA_ROPE)
    wq_ext = jnp.concatenate([wq, _rot_cols(wq[..., MLA_NOPE:])], axis=-1)
    wq_ext = wq_ext.reshape(MLA_Q_LORA, MLA_HEADS * Q_HEAD_COLS).astype(BF16)
    wkv = w_kv_b[l].reshape(MLA_KV_LORA, MLA_HEADS, MLA_NOPE + MLA_V)
    wuk = jnp.transpose(wkv[..., :MLA_NOPE], (1, 2, 0)).astype(BF16)
    wuv = jnp.transpose(wkv[..., MLA_NOPE:], (1, 0, 2)).astype(BF16)
    wo = w_out[l].astype(BF16)
    hgw = HG_HEADS * HG_DV
    return dict(
        norm_mix=norm_mix[l][None], w_in=w_in_ext, hg_onorm=hg_onorm[l][None],
        q_norm=mla_q_norm[l][None], wq=wq_ext, kv_norm=mla_kv_norm[l][None], wuk=wuk, wuv=wuv,
        wo_hg=wo[:hgw], wo_mla=wo[hgw:], norm_ffn=norm_ffn[l][None], w_up=w_up[l].astype(BF16),
        conv_w=conv_w[l], conv_b=conv_b[l][None], w_down=w_down[l].astype(BF16),
        norm_ple=norm_ple[l][None], w_ple_gate=w_ple_gate[l].astype(BF16),
        w_ple_proj=w_ple_proj[l].astype(BF16))


def _mixers(x, w, hg_lower, cs, *, layer, nb, t, hg_s0, attend, sample):
    proj = _in_proj(x, w["norm_mix"], w["w_in"])
    ckv, kr, kcat, q = _mla_proj(proj, cs, w["q_norm"], w["kv_norm"], w["wq"], w["wuk"],
                                 F32 if sample else BF16)
    o_hg, hg_s = _hgrn(proj, hg_lower, w["hg_onorm"], hg_s0, layer=layer, nb=nb, t=t,
                       seq_per_step=min(nb, HG_SAMPLE_SEQS_PER_STEP) if sample else 1,
                       out_dtype=F32 if sample else BF16)
    o_mla = attend(q, ckv, kr, kcat, w["wuv"])
    x1, h2 = _out_proj(o_hg, o_mla, x, w["wo_hg"], w["wo_mla"], w["norm_ffn"])
    return x1, h2, ckv, kr, hg_s


def kernel(x_prompt, x_sample, cache_ckv, cache_krope, state_hgrn, state_conv, page_table,
           p_prompt, p_sample, norm_mix, w_in, hg_lower, hg_onorm, mla_q_norm, w_q_b,
           mla_kv_norm, w_kv_b, w_out, norm_ffn, w_up, conv_w, conv_b, w_down, norm_ple,
           w_ple_gate, w_ple_proj, norm_final):
    depth = w_in.shape[0]
    bp, tp, d = x_prompt.shape
    bs, ts, _ = x_sample.shape
    dff = w_down.shape[1]
    cs_p = _rope_table(jnp.arange(tp, dtype=jnp.int32))
    cs_s = jnp.tile(_rope_table(PAST_LEN + jnp.arange(ts, dtype=jnp.int32)), (bs, 1))
    nf = norm_final[None]
    xp = x_prompt.reshape(bp * tp, d)
    xs = x_sample.reshape(bs * ts, d)
    outs_p, outs_s = [], []
    for l in range(depth):
        w = _layer_weights(l, norm_mix, w_in, hg_onorm, mla_q_norm, w_q_b, mla_kv_norm, w_kv_b,
                           w_out, norm_ffn, w_up, conv_w, conv_b, w_down, norm_ple, w_ple_gate,
                           w_ple_proj)
        final = l == depth - 1

        attend_p = lambda q, ckv, kr, kcat, wuv: _prompt_attn(q, kcat, wuv, nb=bp, t=tp)
        x1, h2, ckv_p, kr_p, hg_p = _mixers(xp, w, hg_lower, cs_p, layer=l, nb=bp, t=tp,
                                            hg_s0=None, attend=attend_p, sample=False)
        x2, tail = _ffn(h2, x1, w["w_up"], w["w_down"], w["conv_w"], w["conv_b"], None,
                        t=tp, t_sample=0)
        cv_p = tail.reshape(bp, -1, SUBLANES, dff)[:, -1, SUBLANES - (CONV_W - 1):]
        xp = _ple(x2, p_prompt[l].reshape(bp * tp, -1), w["norm_ple"], w["w_ple_gate"],
                  w["w_ple_proj"], nf, final=final)
        outs_p.append((ckv_p.reshape(bp, tp, -1), kr_p.reshape(bp, tp, -1), hg_p, cv_p))

        attend_s = lambda q, ckv, kr, kcat, wuv: _sample_attn(
            q, ckv, kr, wuv, cache_ckv, cache_krope, page_table, layer=l, nb=bs, t=ts)
        x1, h2, ckv_s, kr_s, hg_s = _mixers(xs, w, hg_lower, cs_s, layer=l, nb=bs, t=ts,
                                            hg_s0=state_hgrn[l], attend=attend_s, sample=True)
        sp = jnp.pad(state_conv[l], ((0, 0), (ts - (CONV_W - 1), 0), (0, 0))).reshape(bs * ts, dff)
        x2, a_full = _ffn(h2, x1, w["w_up"], w["w_down"], w["conv_w"], w["conv_b"], sp,
                          t=ts, t_sample=ts)
        cv_s = a_full.reshape(bs, ts, dff)[:, ts - (CONV_W - 1):]
        xs = _ple(x2, p_sample[l].reshape(bs * ts, -1), w["norm_ple"], w["w_ple_gate"],
                  w["w_ple_proj"], nf, final=final)
        outs_s.append((ckv_s.reshape(bs, ts, -1), kr_s.reshape(bs, ts, -1), hg_s, cv_s))

    stack = lambda outs, k: jnp.stack([o[k] for o in outs])
    return (xp.reshape(bp, tp, d), xs.reshape(bs, ts, d),
            stack(outs_p, 0), stack(outs_p, 1), stack(outs_p, 2), stack(outs_p, 3),
            stack(outs_s, 0), stack(outs_s, 1), stack(outs_s, 2), stack(outs_s, 3))
```
